```python
import jax, jax.numpy as jnp
from jax import lax
import numpy as np

D_MODEL = 4096
BATCH = 2
SEQ = 8192
DEPTH = 1

N_HEADS = 16
N_KV_HEADS = 4
HEAD_DIM = 128
ATTN_WIDTH = N_HEADS * HEAD_DIM
KV_WIDTH = N_KV_HEADS * HEAD_DIM
WINDOW = 128
Q_BLOCK = 128
CONV_WIDTH = D_MODEL // 2
CONV_KSIZE = 31
N_EXPERTS = 64
TOP_K = 8
N_GROUPS = 8
TOPK_GROUPS = 4
EXPERT_DIM = (D_MODEL * 3) // 16
SHARED_DIM = EXPERT_DIM
ROUTED_SCALE = 2.5
ROW_BLOCK = 256
EPS = 1e-6

IN_SPLITS = [ATTN_WIDTH, KV_WIDTH, KV_WIDTH, CONV_WIDTH, CONV_WIDTH, D_MODEL, D_MODEL]
IN_WIDTH = sum(IN_SPLITS)

kernel_name = "hybrid_gated_swa_conformer_moe_encoder"


def rms_norm(x, g):
    xf = x.astype(jnp.float32)
    y = xf * lax.rsqrt(jnp.mean(xf * xf, axis=-1, keepdims=True) + EPS)
    return (y * g.astype(jnp.float32)).astype(x.dtype)


def layer_norm(x, g, b):
    xf = x.astype(jnp.float32)
    mu = jnp.mean(xf, axis=-1, keepdims=True)
    xc = xf - mu
    y = xc * lax.rsqrt(jnp.mean(xc * xc, axis=-1, keepdims=True) + EPS)
    return (y * g.astype(jnp.float32) + b.astype(jnp.float32)).astype(x.dtype)


def alibi_slopes(n):
    return 2.0 ** (-8.0 * jnp.arange(1, n + 1, dtype=jnp.float32) / n)


def windowed_gqa_attention(q, k, v, sink):
    B, S = q.shape[0], q.shape[1]
    nb = S // Q_BLOCK
    G = N_HEADS // N_KV_HEADS
    qb = q.reshape(B, nb, Q_BLOCK, N_KV_HEADS, G, HEAD_DIM)

    def band(t):
        tp = jnp.pad(t, ((0, 0), (Q_BLOCK, Q_BLOCK), (0, 0), (0, 0)))
        tp = tp.reshape(B, nb + 2, Q_BLOCK, N_KV_HEADS, HEAD_DIM)
        return jnp.concatenate([tp[:, :-2], tp[:, 1:-1], tp[:, 2:]], axis=2)

    kb, vb = band(k), band(v)
    scores = jnp.einsum('bnqhgd,bnkhd->bhgnqk', qb, kb,
                        preferred_element_type=jnp.float32) * (HEAD_DIM ** -0.5)
    q_idx = jnp.arange(Q_BLOCK)
    k_idx = jnp.arange(3 * Q_BLOCK)
    rel = k_idx[None, :] - Q_BLOCK - q_idx[:, None]
    key_pos = jnp.arange(nb)[:, None] * Q_BLOCK - Q_BLOCK + k_idx[None, :]
    valid = (jnp.abs(rel) <= WINDOW)[None] & ((key_pos >= 0) & (key_pos < S))[:, None, :]
    slopes = alibi_slopes(N_HEADS).reshape(N_KV_HEADS, G)
    bias = -slopes[:, :, None, None, None] * jnp.abs(rel).astype(jnp.float32)
    scores = jnp.where(valid, scores + bias, -jnp.inf)
    s_sink = sink.astype(jnp.float32).reshape(N_KV_HEADS, G)[None, :, :, None, None, None]
    m = jnp.maximum(jnp.max(scores, axis=-1, keepdims=True), s_sink)
    e = jnp.exp(scores - m)
    p = e / (jnp.sum(e, axis=-1, keepdims=True) + jnp.exp(s_sink - m))
    out = jnp.einsum('bhgnqk,bnkhd->bnqhgd', p.astype(v.dtype), vb)
    return out.reshape(B, S, ATTN_WIDTH)


def conformer_conv(a, b, w_dw, b_dw, ln_g, ln_b):
    u = a * jax.nn.sigmoid(b)
    pad = CONV_KSIZE // 2
    u = lax.conv_general_dilated(u, w_dw[:, None, :], window_strides=(1,), padding=[(pad, pad)],
                                 dimension_numbers=('NWC', 'WIO', 'NWC'),
                                 feature_group_count=CONV_WIDTH) + b_dw
    return jax.nn.silu(layer_norm(u, ln_g, ln_b))


def route(xt, w_router, router_bias):
    N = xt.shape[0]
    scores = jax.nn.sigmoid((xt @ w_router).astype(jnp.float32))
    choice = scores + router_bias.astype(jnp.float32)
    grp = choice.reshape(N, N_GROUPS, N_EXPERTS // N_GROUPS)
    grp_score = jnp.sum(lax.top_k(grp, 2)[0], axis=-1)
    _, gidx = lax.top_k(grp_score, TOPK_GROUPS)
    gmask = jnp.sum(jax.nn.one_hot(gidx, N_GROUPS, dtype=jnp.float32), axis=-2) > 0
    emask = jnp.repeat(gmask, N_EXPERTS // N_GROUPS, axis=-1)
    _, idx = lax.top_k(jnp.where(emask, choice, -jnp.inf), TOP_K)
    w = jnp.take_along_axis(scores, idx, axis=-1)
    w = w / jnp.sum(w, axis=-1, keepdims=True) * ROUTED_SCALE
    return idx, w


def routed_experts(xt, idx, wts, w_gate, w_up, w_down):
    N, D = xt.shape
    A = N * TOP_K
    e_flat = idx.reshape(-1)
    tok_flat = jnp.repeat(jnp.arange(N, dtype=jnp.int32), TOP_K)
    g_flat = wts.reshape(-1)
    order = jnp.argsort(e_flat)
    e_s, tok_s, g_s = e_flat[order], tok_flat[order], g_flat[order]
    counts = jnp.bincount(e_flat, length=N_EXPERTS)
    padded = (counts + ROW_BLOCK - 1) // ROW_BLOCK * ROW_BLOCK
    pad_end = jnp.cumsum(padded)
    pad_start = pad_end - padded
    start = jnp.cumsum(counts) - counts
    dest = pad_start[e_s] + jnp.arange(A) - start[e_s]
    n_blocks = -(-A // ROW_BLOCK) + N_EXPERTS
    P = n_blocks * ROW_BLOCK
    tok_pad = jnp.full((P,), N, jnp.int32).at[dest].set(tok_s)
    g_pad = jnp.zeros((P,), jnp.float32).at[dest].set(g_s)
    blk_e = jnp.minimum(jnp.searchsorted(pad_end, jnp.arange(n_blocks) * ROW_BLOCK, side='right'),
                        N_EXPERTS - 1)
    x_pad = jnp.concatenate([xt, jnp.zeros((1, D), xt.dtype)], axis=0)

    def step(acc, inp):
        tok, g, e = inp
        xb = x_pad[tok]
        h = jax.nn.silu(xb @ w_gate[e]) * (xb @ w_up[e])
        y = (h @ w_down[e]).astype(jnp.float32) * g[:, None]
        return acc.at[tok].add(y), None

    acc, _ = lax.scan(step, jnp.zeros((N + 1, D), jnp.float32),
                      (tok_pad.reshape(n_blocks, ROW_BLOCK), g_pad.reshape(n_blocks, ROW_BLOCK), blk_e))
    return acc[:N].astype(xt.dtype)


def setup_inputs(seed: int = 0) -> dict:
    key = jax.random.key(seed)
    ks = jax.random.split(key, 24)
    L, D = DEPTH, D_MODEL
    f32 = jnp.float32

    def nrm(k, shape, fan_in):
        return jax.random.normal(k, shape, f32) * (fan_in ** -0.5)

    def gain(k, shape):
        return 1.0 + 0.01 * jax.random.normal(k, shape, f32)

    return {
        "x": jax.random.normal(ks[0], (BATCH, SEQ, D), f32),
        "attn_norm_g": gain(ks[1], (L, D)),
        "w_in": nrm(ks[2], (L, D, IN_WIDTH), D),
        "sink_logits": 0.5 * jax.random.normal(ks[3], (L, N_HEADS), f32),
        "w_o_attn": nrm(ks[4], (L, ATTN_WIDTH, D), ATTN_WIDTH),
        "conv_dw_w": nrm(ks[5], (L, CONV_KSIZE, CONV_WIDTH), CONV_KSIZE),
        "conv_dw_b": 0.02 * jax.random.normal(ks[6], (L, CONV_WIDTH), f32),
        "conv_ln_g": gain(ks[7], (L, CONV_WIDTH)),
        "conv_ln_b": 0.02 * jax.random.normal(ks[8], (L, CONV_WIDTH), f32),
        "w_o_conv": nrm(ks[9], (L, CONV_WIDTH, D), CONV_WIDTH),
        "w_out": nrm(ks[10], (L, D, D), D),
        "ffn_norm_g": gain(ks[11], (L, D)),
        "w_router": nrm(ks[12], (L, D, N_EXPERTS), D),
        "router_bias": 0.01 * jax.random.normal(ks[13], (L, N_EXPERTS), f32),
        "w_exp_gate": nrm(ks[14], (L, N_EXPERTS, D, EXPERT_DIM), D),
        "w_exp_up": nrm(ks[15], (L, N_EXPERTS, D, EXPERT_DIM), D),
        "w_exp_down": nrm(ks[16], (L, N_EXPERTS, EXPERT_DIM, D), EXPERT_DIM),
        "w_sh_gate": nrm(ks[17], (L, D, SHARED_DIM), D),
        "w_sh_up": nrm(ks[18], (L, D, SHARED_DIM), D),
        "w_sh_down": nrm(ks[19], (L, SHARED_DIM, D), SHARED_DIM),
        "final_norm_g": gain(ks[20], (D,)),
    }


def reference(x, attn_norm_g, w_in, sink_logits, w_o_attn, conv_dw_w, conv_dw_b, conv_ln_g, conv_ln_b,
              w_o_conv, w_out, ffn_norm_g, w_router, router_bias, w_exp_gate, w_exp_up, w_exp_down,
              w_sh_gate, w_sh_up, w_sh_down, final_norm_g):
    B, S, D = x.shape
    split_at = [int(v) for v in np.cumsum(IN_SPLITS)[:-1]]
    h = x
    for l in range(DEPTH):
        xn = rms_norm(h, attn_norm_g[l])
        proj = xn @ w_in[l]
        q, k, v, c_val, c_gate, g_attn, g_conv = jnp.split(proj, split_at, axis=-1)
        attn = windowed_gqa_attention(q.reshape(B, S, N_HEADS, HEAD_DIM),
                                      k.reshape(B, S, N_KV_HEADS, HEAD_DIM),
                                      v.reshape(B, S, N_KV_HEADS, HEAD_DIM),
                                      sink_logits[l])
        attn_d = attn @ w_o_attn[l]
        conv = conformer_conv(c_val, c_gate, conv_dw_w[l], conv_dw_b[l], conv_ln_g[l], conv_ln_b[l])
        conv_d = conv @ w_o_conv[l]
        merged = jax.nn.sigmoid(g_attn) * attn_d + jax.nn.sigmoid(g_conv) * conv_d
        h = h + merged @ w_out[l]
        hn = rms_norm(h, ffn_norm_g[l])
        xt = hn.reshape(B * S, D)
        idx, wts = route(xt, w_router[l], router_bias[l])
        routed = routed_experts(xt, idx, wts, w_exp_gate[l], w_exp_up[l], w_exp_down[l])
        shared = (jax.nn.silu(xt @ w_sh_gate[l]) * (xt @ w_sh_up[l])) @ w_sh_down[l]
        h = h + (routed + shared).reshape(B, S, D)
    return rms_norm(h, final_norm_g)
```

```python
import functools

import jax
import jax.numpy as jnp
from jax import lax
from jax.experimental import pallas as pl
from jax.experimental.pallas import tpu as pltpu

F32, BF16, U32, I32 = jnp.float32, jnp.bfloat16, jnp.uint32, jnp.int32

EPS = 1e-6
WINDOW = 128
Q_BLOCK = 128
TOP_K = 8
N_GROUPS = 8
TOPK_GROUPS = 4
ROUTED_SCALE = 2.5

V7X_VMEM_BYTES = 64 * 1024 * 1024
V7X_BF16_SUBLANES = 16
MIB = 1024 * 1024
VMEM_CEILING_BYTES = V7X_VMEM_BYTES - 6 * MIB
VMEM_TEMPORARIES_MIB = 16
NORM_ROWS = 32


def _compiler_params(semantics, window_mib):
    return pltpu.CompilerParams(
        dimension_semantics=semantics,
        vmem_limit_bytes=min(int((window_mib + VMEM_TEMPORARIES_MIB) * MIB), VMEM_CEILING_BYTES),
    )


def _tile(dim, target):
    t = min(dim, target)
    while dim % t:
        t //= 2
    return t


def _sigmoid(v):
    return 1.0 / (1.0 + jnp.exp(-v))


def _pack_halves(v_f32):
    bits = lax.bitcast_convert_type(v_f32, U32)
    h = v_f32.shape[1] // 2
    return (bits[:, :h] >> 16) | bits[:, h:]


def _unpack_halves(p_u32):
    lo = lax.bitcast_convert_type(p_u32 << 16, F32)
    hi = lax.bitcast_convert_type(p_u32 & jnp.uint32(0xFFFF0000), F32)
    return lo, hi


def _inproj_body(x_ref, g_ref, w_ref, o_ref, xn_ref):
    @pl.when(pl.program_id(1) == 0)
    def _():
        def norm_rows(c, carry):
            rows = pl.ds(pl.multiple_of(c * NORM_ROWS, NORM_ROWS), NORM_ROWS)
            x = x_ref[rows, :]
            ms = jnp.mean(x * x, axis=-1, keepdims=True)
            xn_ref[rows, :] = (x * lax.rsqrt(ms + EPS) * g_ref[...]).astype(BF16)
            return carry

        lax.fori_loop(0, x_ref.shape[0] // NORM_ROWS, norm_rows, 0)

    o_ref[...] = jnp.dot(xn_ref[...], w_ref[...], preferred_element_type=F32).astype(o_ref.dtype)


def _in_projection(x2, g, w):
    n, d = x2.shape
    wid = w.shape[1]
    tm, tn = _tile(n, 512), _tile(wid, 1024)
    vmem = (2 * tm * d * 4 + tm * d * 2 + 2 * d * tn * 2 + 2 * tm * tn * 2) / MIB
    return pl.pallas_call(
        _inproj_body,
        out_shape=jax.ShapeDtypeStruct((n, wid), BF16),
        grid=(n // tm, wid // tn),
        in_specs=[
            pl.BlockSpec((tm, d), lambda i, j: (i, 0)),
            pl.BlockSpec((1, d), lambda i, j: (0, 0)),
            pl.BlockSpec((d, tn), lambda i, j: (0, j)),
        ],
        out_specs=pl.BlockSpec((tm, tn), lambda i, j: (i, j)),
        scratch_shapes=[pltpu.VMEM((tm, d), BF16)],
        compiler_params=_compiler_params(("parallel", "arbitrary"), vmem),
        name="in_projection",
    )(x2, g, w)


def _attn_body(sink_ref, q_ref, kp_ref, kc_ref, kn_ref, vp_ref, vc_ref, vn_ref, o_ref, *,
               blocks_per_seq, n_heads, n_kv_heads, head_dim):
    tq = Q_BLOCK
    nb = pl.program_id(0) % blocks_per_seq
    k = jnp.concatenate([kp_ref[...], kc_ref[...], kn_ref[...]], axis=0)
    v = jnp.concatenate([vp_ref[...], vc_ref[...], vn_ref[...]], axis=0)
    qi = lax.broadcasted_iota(I32, (tq, 3 * tq), 0)
    ki = lax.broadcasted_iota(I32, (tq, 3 * tq), 1)
    rel = ki - tq - qi
    valid = (jnp.abs(rel) <= WINDOW)
    valid = valid & ((ki >= tq) | (nb > 0)) & ((ki < 2 * tq) | (nb < blocks_per_seq - 1))
    dist = jnp.abs(rel).astype(F32)
    group = n_heads // n_kv_heads
    scale = head_dim ** -0.5
    for h in range(n_heads):
        j = h // group
        slope = 2.0 ** (-8.0 * (h + 1) / n_heads)
        qh = q_ref[:, h * head_dim:(h + 1) * head_dim]
        kj = k[:, j * head_dim:(j + 1) * head_dim]
        s = lax.dot_general(qh, kj, (((1,), (1,)), ((), ())), preferred_element_type=F32)
        s = jnp.where(valid, s * scale - slope * dist, -jnp.inf)
        sink = sink_ref[h]
        m = jnp.maximum(jnp.max(s, axis=-1, keepdims=True), sink)
        e = jnp.exp(s - m)
        denom = jnp.sum(e, axis=-1, keepdims=True) + jnp.exp(sink - m)
        pv = jnp.dot(e.astype(BF16), v[:, j * head_dim:(j + 1) * head_dim], preferred_element_type=F32)
        o_ref[:, h * head_dim:(h + 1) * head_dim] = (pv * (1.0 / denom)).astype(o_ref.dtype)


def _window_attention(proj, sink, *, seq, attn_width, kv_width, k_off, v_off):
    n = proj.shape[0]
    n_heads = sink.shape[0]
    head_dim = attn_width // n_heads
    n_kv_heads = kv_width // head_dim
    tq = Q_BLOCK
    nblk = n // tq
    kb, vb = k_off // kv_width, v_off // kv_width
    body = functools.partial(_attn_body, blocks_per_seq=seq // tq, n_heads=n_heads,
                             n_kv_heads=n_kv_heads, head_dim=head_dim)

    def band(col):
        return [
            pl.BlockSpec((tq, kv_width), lambda i: (jnp.maximum(i - 1, 0), col)),
            pl.BlockSpec((tq, kv_width), lambda i: (i, col)),
            pl.BlockSpec((tq, kv_width), lambda i: (jnp.minimum(i + 1, nblk - 1), col)),
        ]

    return pl.pallas_call(
        body,
        out_shape=jax.ShapeDtypeStruct((n, attn_width), BF16),
        grid=(nblk,),
        in_specs=[pl.BlockSpec(memory_space=pltpu.SMEM),
                  pl.BlockSpec((tq, attn_width), lambda i: (i, 0))] + band(kb) + band(vb),
        out_specs=pl.BlockSpec((tq, attn_width), lambda i: (i, 0)),
        compiler_params=_compiler_params(("parallel",), 4),
        name="window_attention",
    )(sink, proj, proj, proj, proj, proj, proj, proj)


CONV_ROWS = 32
CONV_LANES = 512


def _conv_body(ap_ref, ac_ref, an_ref, bp_ref, bc_ref, bn_ref, w_ref, bias_ref, lg_ref, lb_ref,
               o_ref, u_ref, y_ref, *, tiles_per_seq, tt, c, ksize):
    ti = pl.program_id(0) % tiles_per_seq
    halo = V7X_BF16_SUBLANES

    def glu(a_ref, b_ref):
        return a_ref[...].astype(F32) * _sigmoid(b_ref[...].astype(F32))

    u_ref[0:halo, :] = jnp.where(ti > 0, glu(ap_ref, bp_ref), 0.0)
    u_ref[halo:halo + tt, :] = glu(ac_ref, bc_ref)
    u_ref[halo + tt:, :] = jnp.where(ti < tiles_per_seq - 1, glu(an_ref, bn_ref), 0.0)

    pad = ksize // 2
    cl, rr = min(c, CONV_LANES), min(tt, CONV_ROWS)
    for c0 in range(0, c, cl):
        for r0 in range(0, tt, rr):
            acc = jnp.zeros((rr, cl), F32)
            for j in range(ksize):
                s = halo - pad + r0 + j
                acc = acc + u_ref[s:s + rr, c0:c0 + cl] * w_ref[j:j + 1, c0:c0 + cl]
            y_ref[r0:r0 + rr, c0:c0 + cl] = acc + bias_ref[:, c0:c0 + cl]

    y = y_ref[...]
    mu = jnp.mean(y, axis=-1, keepdims=True)
    yc = y - mu
    var = jnp.mean(yc * yc, axis=-1, keepdims=True)
    z = yc * lax.rsqrt(var + EPS) * lg_ref[...] + lb_ref[...]
    o_ref[...] = (z * _sigmoid(z)).astype(o_ref.dtype)


def _conformer_conv(proj, w_dw, b_dw, ln_g, ln_b, *, seq, a_off, b_off):
    n = proj.shape[0]
    ksize, c = w_dw.shape
    halo = V7X_BF16_SUBLANES
    assert ksize // 2 <= halo
    tt = _tile(seq, 128)
    ab, bb = a_off // c, b_off // c
    hb = tt // halo
    nhalo = n // halo
    body = functools.partial(_conv_body, tiles_per_seq=seq // tt, tt=tt, c=c, ksize=ksize)

    def band(col):
        return [
            pl.BlockSpec((halo, c), lambda i: (jnp.maximum(i * hb - 1, 0), col)),
            pl.BlockSpec((tt, c), lambda i: (i, col)),
            pl.BlockSpec((halo, c), lambda i: (jnp.minimum((i + 1) * hb, nhalo - 1), col)),
        ]

    row = lambda a: a.reshape(1, c)
    vec = pl.BlockSpec((1, c), lambda i: (0, 0))
    return pl.pallas_call(
        body,
        out_shape=jax.ShapeDtypeStruct((n, c), BF16),
        grid=(n // tt,),
        in_specs=band(ab) + band(bb) + [pl.BlockSpec((ksize, c), lambda i: (0, 0)), vec, vec, vec],
        out_specs=pl.BlockSpec((tt, c), lambda i: (i, 0)),
        scratch_shapes=[pltpu.VMEM((tt + 2 * halo, c), F32), pltpu.VMEM((tt, c), F32)],
        compiler_params=_compiler_params(("parallel",), 8),
        name="conformer_conv",
    )(proj, proj, proj, proj, proj, proj, w_dw, row(b_dw), row(ln_g), row(ln_b))


def _merge_body(attn_ref, conv_ref, woa_ref, woc_ref, ga_ref, gc_ref, o_ref):
    ad = jnp.dot(attn_ref[...], woa_ref[...], preferred_element_type=F32)
    cd = jnp.dot(conv_ref[...], woc_ref[...], preferred_element_type=F32)
    merged = _sigmoid(ga_ref[...].astype(F32)) * ad + _sigmoid(gc_ref[...].astype(F32)) * cd
    o_ref[...] = merged.astype(o_ref.dtype)


def _gated_merge(attn, conv, w_oa, w_oc, proj, *, ga_off, gc_off):
    n, aw = attn.shape
    cw = conv.shape[1]
    d = w_oa.shape[1]
    tm, tn = _tile(n, 512), _tile(d, 1024)
    gab, gcb = ga_off // tn, gc_off // tn
    vmem = (2 * tm * (aw + cw) * 2 + 2 * (aw + cw) * tn * 2 + 6 * tm * tn * 2) / MIB
    return pl.pallas_call(
        _merge_body,
        out_shape=jax.ShapeDtypeStruct((n, d), BF16),
        grid=(n // tm, d // tn),
        in_specs=[
            pl.BlockSpec((tm, aw), lambda i, j: (i, 0)),
            pl.BlockSpec((tm, cw), lambda i, j: (i, 0)),
            pl.BlockSpec((aw, tn), lambda i, j: (0, j)),
            pl.BlockSpec((cw, tn), lambda i, j: (0, j)),
            pl.BlockSpec((tm, tn), lambda i, j: (i, gab + j)),
            pl.BlockSpec((tm, tn), lambda i, j: (i, gcb + j)),
        ],
        out_specs=pl.BlockSpec((tm, tn), lambda i, j: (i, j)),
        compiler_params=_compiler_params(("parallel", "parallel"), vmem),
        name="gated_merge",
    )(attn, conv, w_oa, w_oc, proj, proj)


def _outproj_body(m_ref, w_ref, x_ref, o_ref):
    o_ref[...] = x_ref[...] + jnp.dot(m_ref[...], w_ref[...], preferred_element_type=F32)


def _out_projection(merged, w_out, x2):
    n, d = merged.shape
    tm, tn = _tile(n, 512), _tile(d, 1024)
    vmem = (2 * tm * d * 2 + 2 * d * tn * 2 + 4 * tm * tn * 4) / MIB
    return pl.pallas_call(
        _outproj_body,
        out_shape=jax.ShapeDtypeStruct((n, d), F32),
        grid=(n // tm, d // tn),
        in_specs=[
            pl.BlockSpec((tm, d), lambda i, j: (i, 0)),
            pl.BlockSpec((d, tn), lambda i, j: (0, j)),
            pl.BlockSpec((tm, tn), lambda i, j: (i, j)),
        ],
        out_specs=pl.BlockSpec((tm, tn), lambda i, j: (i, j)),
        compiler_params=_compiler_params(("parallel", "parallel"), vmem),
        name="out_projection",
    )(merged, w_out, x2)


def _router_body(h_ref, g_ref, wrt_ref, rb_ref, wsg_ref, wsu_ref, wsd_ref,
                 xp_ref, h2_ref, eidx_ref, rank_ref, wt_ref, cnt_ref, carry_ref, hn_ref, hb_ref,
                 *, n_exp, tm):
    @pl.when(pl.program_id(0) == 0)
    def _():
        carry_ref[...] = jnp.zeros_like(carry_ref)

    def norm_rows(c, carry):
        rows = pl.ds(pl.multiple_of(c * NORM_ROWS, NORM_ROWS), NORM_ROWS)
        h = h_ref[rows, :]
        ms = jnp.mean(h * h, axis=-1, keepdims=True)
        hn = h * lax.rsqrt(ms + EPS) * g_ref[...]
        hb = hn.astype(BF16)
        hn_ref[rows, :] = hn
        hb_ref[rows, :] = hb
        xp_ref[rows, :] = _pack_halves(hb.astype(F32))
        return carry

    lax.fori_loop(0, tm // NORM_ROWS, norm_rows, 0)

    hb = hb_ref[...]
    gate = jnp.dot(hb, wsg_ref[...], preferred_element_type=F32)
    up = jnp.dot(hb, wsu_ref[...], preferred_element_type=F32)
    act = (gate * _sigmoid(gate) * up).astype(BF16)
    h2_ref[...] = h_ref[...] + jnp.dot(act, wsd_ref[...], preferred_element_type=F32)

    logits = lax.dot_general(wrt_ref[...], hn_ref[...], (((1,), (1,)), ((), ())),
                             precision=lax.Precision.HIGHEST, preferred_element_type=F32)
    score = _sigmoid(logits)
    choice = score + rb_ref[...]

    gsz = n_exp // N_GROUPS
    sub = lax.broadcasted_iota(I32, (gsz, tm), 0)
    groups, gscore = [], []
    for g in range(N_GROUPS):
        grp = choice[g * gsz:(g + 1) * gsz, :]
        m1 = jnp.max(grp, axis=0, keepdims=True)
        first = jnp.min(jnp.where(grp == m1, sub, gsz), axis=0, keepdims=True)
        m2 = jnp.max(jnp.where(sub == first, -jnp.inf, grp), axis=0, keepdims=True)
        groups.append(grp)
        gscore.append(m1 + m2)
    kept = []
    for g in range(N_GROUPS):
        beaten_by = jnp.zeros((1, tm), I32)
        for o in range(N_GROUPS):
            if o != g:
                wins = (gscore[o] >= gscore[g]) if o < g else (gscore[o] > gscore[g])
                beaten_by = beaten_by + wins.astype(I32)
        kept.append(jnp.where(beaten_by < TOPK_GROUPS, groups[g], -jnp.inf))
    cand = jnp.concatenate(kept, axis=0)

    eiota = lax.broadcasted_iota(I32, (n_exp, tm), 0)
    msel = jnp.zeros((n_exp, tm), F32)
    sel_idx, sel_score = [], []
    for _ in range(TOP_K):
        m = jnp.max(cand, axis=0, keepdims=True)
        first = jnp.min(jnp.where(cand == m, eiota, n_exp), axis=0, keepdims=True)
        hit = eiota == first
        sel_idx.append(first)
        sel_score.append(jnp.sum(jnp.where(hit, score, 0.0), axis=0, keepdims=True))
        msel = msel + hit.astype(F32)
        cand = jnp.where(hit, -jnp.inf, cand)

    rows = lax.broadcasted_iota(I32, (tm, tm), 0)
    cols = lax.broadcasted_iota(I32, (tm, tm), 1)
    earlier = (rows < cols).astype(BF16)
    before = jnp.dot(msel.astype(BF16), earlier, preferred_element_type=F32)
    rank_full = carry_ref[:, 0:1] + before
    total = carry_ref[...] + jnp.sum(msel, axis=1, keepdims=True)
    carry_ref[...] = total
    cnt_ref[...] = total

    wsum = sel_score[0]
    for k in range(1, TOP_K):
        wsum = wsum + sel_score[k]
    ranks = [jnp.sum(jnp.where(eiota == sel_idx[k], rank_full, 0.0), axis=0, keepdims=True)
             for k in range(TOP_K)]
    eidx_ref[...] = jnp.concatenate(sel_idx, axis=0)
    rank_ref[...] = jnp.concatenate(ranks, axis=0).astype(I32)
    wt_ref[...] = jnp.concatenate([sc / wsum * ROUTED_SCALE for sc in sel_score], axis=0)


def _router_shared(h1, g, w_router_t, router_bias, wsg, wsu, wsd):
    n, d = h1.shape
    n_exp = w_router_t.shape[0]
    f = wsg.shape[1]
    tm = _tile(n, 256)
    body = functools.partial(_router_body, n_exp=n_exp, tm=tm)
    const = lambda shape: pl.BlockSpec(shape, lambda i: (0,) * len(shape), pipeline_mode=pl.Buffered(1))
    vmem = (4 * tm * d * 4 + 2 * tm * d * 2 + 3 * d * f * 2 + n_exp * d * 4 + tm * d * 6) / MIB
    return pl.pallas_call(
        body,
        out_shape=(
            jax.ShapeDtypeStruct((n, d // 2), U32),
            jax.ShapeDtypeStruct((n, d), F32),
            jax.ShapeDtypeStruct((TOP_K, n), I32),
            jax.ShapeDtypeStruct((TOP_K, n), I32),
            jax.ShapeDtypeStruct((TOP_K, n), F32),
            jax.ShapeDtypeStruct((n_exp, 128), F32),
        ),
        grid=(n // tm,),
        in_specs=[
            pl.BlockSpec((tm, d), lambda i: (i, 0)),
            const((1, d)),
            const((n_exp, d)),
            const((n_exp, 1)),
            const((d, f)),
            const((d, f)),
            const((f, d)),
        ],
        out_specs=(
            pl.BlockSpec((tm, d // 2), lambda i: (i, 0)),
            pl.BlockSpec((tm, d), lambda i: (i, 0)),
            pl.BlockSpec((TOP_K, tm), lambda i: (0, i)),
            pl.BlockSpec((TOP_K, tm), lambda i: (0, i)),
            pl.BlockSpec((TOP_K, tm), lambda i: (0, i)),
            pl.BlockSpec((n_exp, 128), lambda i: (0, 0)),
        ),
        scratch_shapes=[pltpu.VMEM((n_exp, 128), F32), pltpu.VMEM((tm, d), F32), pltpu.VMEM((tm, d), BF16)],
        compiler_params=_compiler_params(("arbitrary",), vmem),
        name="router_shared",
    )(h1, g, w_router_t, router_bias, wsg, wsu, wsd)


def _dispatch_body(slot_ref, xp_ref, xs_ref, sem, *, tt):
    def row_copies(r):
        return [pltpu.make_async_copy(xp_ref.at[pl.ds(r, 1)],
                                      xs_ref.at[pl.ds(slot_ref[r * TOP_K + k], 1)], sem)
                for k in range(TOP_K)]

    def start(r, carry):
        for cp in row_copies(r):
            cp.start()
        return carry

    def wait(r, carry):
        for cp in row_copies(r):
            cp.wait()
        return carry

    lax.fori_loop(0, tt, start, 0)
    lax.fori_loop(0, tt, wait, 0)


def _moe_dispatch(slot_flat, xp):
    n, half = xp.shape
    tt = _tile(n, 256)
    return pl.pallas_call(
        functools.partial(_dispatch_body, tt=tt),
        out_shape=jax.ShapeDtypeStruct((n * TOP_K, half), U32),
        grid=(n // tt,),
        in_specs=[
            pl.BlockSpec((tt * TOP_K,), lambda i: (i,), memory_space=pltpu.SMEM),
            pl.BlockSpec((tt, half), lambda i: (i, 0)),
        ],
        out_specs=pl.BlockSpec(memory_space=pl.ANY),
        scratch_shapes=[pltpu.SemaphoreType.DMA(())],
        compiler_params=_compiler_params(("arbitrary",), 2 * tt * half * 4 / MIB),
        name="moe_dispatch",
    )(slot_flat, xp)


def _ffn_body(vblk_ref, vexp_ref, vlo_ref, vhi_ref, xs_ref, wg_ref, wu_ref, wd_ref, ys_ref, *, rb):
    del vexp_ref
    v = pl.program_id(0)
    lo, hi = _unpack_halves(xs_ref[...])
    x = jnp.concatenate([lo.astype(BF16), hi.astype(BF16)], axis=1)
    gate = jnp.dot(x, wg_ref[0], preferred_element_type=F32)
    up = jnp.dot(x, wu_ref[0], preferred_element_type=F32)
    act = (gate * _sigmoid(gate) * up).astype(BF16)
    y = jnp.dot(act, wd_ref[0], preferred_element_type=F32)
    packed = _pack_halves(y.astype(BF16).astype(F32))
    first_visit = (v == 0) | (vblk_ref[v] != vblk_ref[jnp.maximum(v - 1, 0)])

    @pl.when(first_visit)
    def _():
        ys_ref[...] = packed

    @pl.when(jnp.logical_not(first_visit))
    def _():
        rows = vblk_ref[v] * rb + lax.broadcasted_iota(I32, packed.shape, 0)
        mine = (rows >= vlo_ref[v]) & (rows < vhi_ref[v])
        ys_ref[...] = jnp.where(mine, packed, ys_ref[...])


def _expert_ffn(visits, xs, wg, wu, wd, *, rb):
    rows, half = xs.shape
    n_exp, d, f = wg.shape
    n_visits = visits[0].shape[0]
    vmem = (2 * 3 * d * f * 2 + 4 * rb * half * 4) / MIB
    grid_spec = pltpu.PrefetchScalarGridSpec(
        num_scalar_prefetch=4,
        grid=(n_visits,),
        in_specs=[
            pl.BlockSpec((rb, half), lambda v, vb, ve, lo, hi: (vb[v], 0)),
            pl.BlockSpec((1, d, f), lambda v, vb, ve, lo, hi: (ve[v], 0, 0)),
            pl.BlockSpec((1, d, f), lambda v, vb, ve, lo, hi: (ve[v], 0, 0)),
            pl.BlockSpec((1, f, d), lambda v, vb, ve, lo, hi: (ve[v], 0, 0)),
        ],
        out_specs=pl.BlockSpec((rb, half), lambda v, vb, ve, lo, hi: (vb[v], 0)),
    )
    return pl.pallas_call(
        functools.partial(_ffn_body, rb=rb),
        out_shape=jax.ShapeDtypeStruct((rows, half), U32),
        grid_spec=grid_spec,
        compiler_params=_compiler_params(("arbitrary",), vmem),
        name="expert_ffn",
    )(*visits, xs, wg, wu, wd)


def _visit_tables(counts, n_rows, rb):
    n_exp = counts.shape[0]
    n_blocks = n_rows // rb
    n_visits = n_blocks + n_exp
    ends = jnp.cumsum(counts)
    starts = ends - counts
    first_blk = starts // rb
    last_blk = jnp.maximum(ends - 1, starts) // rb
    per_exp = jnp.where(counts > 0, last_blk - first_blk + 1, 0)
    vis_end = jnp.cumsum(per_exp)
    vis_start = vis_end - per_exp
    total = vis_end[-1]
    v = jnp.arange(n_visits, dtype=I32)
    e = jnp.minimum(jnp.searchsorted(vis_end, v, side="right"), n_exp - 1).astype(I32)
    blk = first_blk[e] + (v - vis_start[e])
    lo = jnp.maximum(starts[e], blk * rb)
    hi = jnp.minimum(ends[e], (blk + 1) * rb)
    used = v < total
    last_e = jnp.max(jnp.where(counts > 0, jnp.arange(n_exp, dtype=I32), 0))
    blk = jnp.where(used, blk, n_blocks - 1).astype(I32)
    e = jnp.where(used, e, last_e).astype(I32)
    lo = jnp.where(used, lo, 0).astype(I32)
    hi = jnp.where(used, hi, 0).astype(I32)
    return blk, e, lo, hi


def _combine_body(slot_ref, w_ref, h2_ref, ys_ref, g_ref, o_ref, ybuf, sem, *, tt, d):
    half = d // 2

    def row_copies(r):
        return [pltpu.make_async_copy(ys_ref.at[pl.ds(slot_ref[r * TOP_K + k], 1)],
                                      ybuf.at[k, pl.ds(r, 1)], sem)
                for k in range(TOP_K)]

    def start(r, carry):
        for cp in row_copies(r):
            cp.start()
        return carry

    def wait(r, carry):
        for cp in row_copies(r):
            cp.wait()
        return carry

    lax.fori_loop(0, tt, start, 0)
    lax.fori_loop(0, tt, wait, 0)

    acc_lo = h2_ref[:, :half]
    acc_hi = h2_ref[:, half:]
    for k in range(TOP_K):
        lo, hi = _unpack_halves(ybuf[k])
        wk = w_ref[:, k:k + 1]
        acc_lo = acc_lo + wk * lo
        acc_hi = acc_hi + wk * hi
    ss = jnp.sum(acc_lo * acc_lo, axis=-1, keepdims=True) + jnp.sum(acc_hi * acc_hi, axis=-1, keepdims=True)
    inv = lax.rsqrt(ss / d + EPS)
    o_ref[:, :half] = acc_lo * inv * g_ref[:, :half]
    o_ref[:, half:] = acc_hi * inv * g_ref[:, half:]


def _moe_combine(slot_flat, w_tok, h2, ys, g):
    n, d = h2.shape
    half = d // 2
    tt = _tile(n, 128)
    vmem = (TOP_K * tt * half * 4 + 4 * tt * d * 4) / MIB
    return pl.pallas_call(
        functools.partial(_combine_body, tt=tt, d=d),
        out_shape=jax.ShapeDtypeStruct((n, d), F32),
        grid=(n // tt,),
        in_specs=[
            pl.BlockSpec((tt * TOP_K,), lambda i: (i,), memory_space=pltpu.SMEM),
            pl.BlockSpec((tt, TOP_K), lambda i: (i, 0)),
            pl.BlockSpec((tt, d), lambda i: (i, 0)),
            pl.BlockSpec(memory_space=pl.ANY),
            pl.BlockSpec((1, d), lambda i: (0, 0)),
        ],
        out_specs=pl.BlockSpec((tt, d), lambda i: (i, 0)),
        scratch_shapes=[pltpu.VMEM((TOP_K, tt, half), U32), pltpu.SemaphoreType.DMA(())],
        compiler_params=_compiler_params(("arbitrary",), vmem),
        name="moe_combine",
    )(slot_flat, w_tok, h2, ys, g)


def _layer(h, l, p, dims):
    b, s, d = h.shape
    n = b * s
    x2 = h.reshape(n, d)
    aw, kvw, cw = dims["attn"], dims["kv"], dims["conv"]
    q0, k0, v0, cv0, cg0, ga0, gc0 = dims["src_offsets"]
    w_in = p["w_in"][l]
    pieces = [(q0, aw), (cv0, cw), (cg0, cw), (ga0, d), (gc0, d), (k0, kvw), (v0, kvw)]
    w_in_b = jnp.concatenate([w_in[:, o:o + wd].astype(BF16) for o, wd in pieces], axis=1)
    offs = [0]
    for _, wd in pieces:
        offs.append(offs[-1] + wd)
    cv_off, cg_off, ga_off, gc_off, k_off, v_off = offs[1:7]

    proj = _in_projection(x2, p["attn_norm_g"][l].reshape(1, d), w_in_b)
    attn = _window_attention(proj, p["sink_logits"][l], seq=s, attn_width=aw, kv_width=kvw,
                             k_off=k_off, v_off=v_off)
    conv = _conformer_conv(proj, p["conv_dw_w"][l], p["conv_dw_b"][l], p["conv_ln_g"][l],
                           p["conv_ln_b"][l], seq=s, a_off=cv_off, b_off=cg_off)
    merged = _gated_merge(attn, conv, p["w_o_attn"][l].astype(BF16), p["w_o_conv"][l].astype(BF16),
                          proj, ga_off=ga_off, gc_off=gc_off)
    h1 = _out_projection(merged, p["w_out"][l].astype(BF16), x2)

    n_exp = p["w_router"].shape[-1]
    xp, h2, eidx, rank, wts, cnt = _router_shared(
        h1, p["ffn_norm_g"][l].reshape(1, d), p["w_router"][l].T, p["router_bias"][l].reshape(n_exp, 1),
        p["w_sh_gate"][l].astype(BF16), p["w_sh_up"][l].astype(BF16), p["w_sh_down"][l].astype(BF16))

    counts = cnt[:, 0].astype(I32)
    starts = jnp.cumsum(counts) - counts
    slot_flat = (starts[eidx] + rank).T.reshape(-1)
    rb = _tile(n * TOP_K, 256)
    visits = _visit_tables(counts, n * TOP_K, rb)

    xs = _moe_dispatch(slot_flat, xp)
    ys = _expert_ffn(visits, xs, p["w_exp_gate"][l].astype(BF16), p["w_exp_up"][l].astype(BF16),
                     p["w_exp_down"][l].astype(BF16), rb=rb)
    return slot_flat, wts.T, h2, ys


def kernel(x, attn_norm_g, w_in, sink_logits, w_o_attn, conv_dw_w, conv_dw_b, conv_ln_g, conv_ln_b,
           w_o_conv, w_out, ffn_norm_g, w_router, router_bias, w_exp_gate, w_exp_up, w_exp_down,
           w_sh_gate, w_sh_up, w_sh_down, final_norm_g):
    b, s, d = x.shape
    depth = w_in.shape[0]
    assert depth == 1, "the final RMSNorm is fused into the only layer's MoE combine"
    aw = w_o_attn.shape[1]
    cw = conv_dw_w.shape[2]
    kvw = (w_in.shape[2] - aw - 2 * cw - 2 * d) // 2
    assert s % Q_BLOCK == 0 and WINDOW <= Q_BLOCK
    src = [0, aw, aw + kvw, aw + 2 * kvw, aw + 2 * kvw + cw, aw + 2 * kvw + 2 * cw, aw + 2 * kvw + 2 * cw + d]
    dims = {"attn": aw, "kv": kvw, "conv": cw, "src_offsets": src}
    p = dict(attn_norm_g=attn_norm_g, w_in=w_in, sink_logits=sink_logits, w_o_attn=w_o_attn,
             conv_dw_w=conv_dw_w, conv_dw_b=conv_dw_b, conv_ln_g=conv_ln_g, conv_ln_b=conv_ln_b,
             w_o_conv=w_o_conv, w_out=w_out, ffn_norm_g=ffn_norm_g, w_router=w_router,
             router_bias=router_bias, w_exp_gate=w_exp_gate, w_exp_up=w_exp_up, w_exp_down=w_exp_down,
             w_sh_gate=w_sh_gate, w_sh_up=w_sh_up, w_sh_down=w_sh_down)
    slot_flat, w_tok, h2, ys = _layer(x, 0, p, dims)
    out = _moe_combine(slot_flat, w_tok, h2, ys, final_norm_g.reshape(1, d))
    return out.reshape(b, s, d)
```

```python
import functools

import jax
import jax.numpy as jnp
from jax import lax
from jax.experimental import pallas as pl
from jax.experimental.pallas import tpu as pltpu

F32, BF16, U32, I32 = jnp.float32, jnp.bfloat16, jnp.uint32, jnp.int32

EPS = 1e-6
WINDOW = 128
Q_BLOCK = 128
TOP_K = 8
N_GROUPS = 8
TOPK_GROUPS = 4
ROUTED_SCALE = 2.5

V7X_VMEM_BYTES = 64 * 1024 * 1024
V7X_BF16_SUBLANES = 16
MIB = 1024 * 1024
VMEM_CEILING_BYTES = V7X_VMEM_BYTES - 6 * MIB
VMEM_TEMPORARIES_MIB = 16
NORM_ROWS = 32
DMA_PRIORITIES = 2


def _compiler_params(semantics, window_mib):
    return pltpu.CompilerParams(
        dimension_semantics=semantics,
        vmem_limit_bytes=min(int((window_mib + VMEM_TEMPORARIES_MIB) * MIB), VMEM_CEILING_BYTES),
    )


def _tile(dim, target):
    t = min(dim, target)
    while dim % t:
        t //= 2
    return t


def _sigmoid(v):
    return 1.0 / (1.0 + jnp.exp(-v))


def _pack_halves(v_f32):
    bits = lax.bitcast_convert_type(v_f32, U32)
    h = v_f32.shape[1] // 2
    return (bits[:, :h] >> 16) | bits[:, h:]


def _unpack_halves(p_u32):
    lo = lax.bitcast_convert_type(p_u32 << 16, F32)
    hi = lax.bitcast_convert_type(p_u32 & jnp.uint32(0xFFFF0000), F32)
    return lo, hi


def _inproj_body(x_ref, g_ref, w_ref, o_ref, xn_ref):
    @pl.when(pl.program_id(1) == 0)
    def _():
        def norm_rows(c, carry):
            rows = pl.ds(pl.multiple_of(c * NORM_ROWS, NORM_ROWS), NORM_ROWS)
            x = x_ref[rows, :]
            ms = jnp.mean(x * x, axis=-1, keepdims=True)
            xn_ref[rows, :] = (x * lax.rsqrt(ms + EPS) * g_ref[...]).astype(BF16)
            return carry

        lax.fori_loop(0, x_ref.shape[0] // NORM_ROWS, norm_rows, 0)

    o_ref[...] = jnp.dot(xn_ref[...], w_ref[...], preferred_element_type=F32).astype(o_ref.dtype)


def _in_projection(x2, g, w):
    n, d = x2.shape
    wid = w.shape[1]
    tm, tn = _tile(n, 512), _tile(wid, 1024)
    vmem = (2 * tm * d * 4 + tm * d * 2 + 2 * d * tn * 2 + 2 * tm * tn * 2) / MIB
    return pl.pallas_call(
        _inproj_body,
        out_shape=jax.ShapeDtypeStruct((n, wid), BF16),
        grid=(n // tm, wid // tn),
        in_specs=[
            pl.BlockSpec((tm, d), lambda i, j: (i, 0)),
            pl.BlockSpec((1, d), lambda i, j: (0, 0)),
            pl.BlockSpec((d, tn), lambda i, j: (0, j)),
        ],
        out_specs=pl.BlockSpec((tm, tn), lambda i, j: (i, j)),
        scratch_shapes=[pltpu.VMEM((tm, d), BF16)],
        compiler_params=_compiler_params(("parallel", "arbitrary"), vmem),
        name="in_projection",
    )(x2, g, w)


def _attn_body(sink_ref, q_ref, kp_ref, kc_ref, kn_ref, vp_ref, vc_ref, vn_ref, o_ref, *,
               blocks_per_seq, n_heads, n_kv_heads, head_dim):
    tq = Q_BLOCK
    nb = pl.program_id(0) % blocks_per_seq
    k = jnp.concatenate([kp_ref[...], kc_ref[...], kn_ref[...]], axis=0)
    v = jnp.concatenate([vp_ref[...], vc_ref[...], vn_ref[...]], axis=0)
    qi = lax.broadcasted_iota(I32, (tq, 3 * tq), 0)
    ki = lax.broadcasted_iota(I32, (tq, 3 * tq), 1)
    rel = ki - tq - qi
    valid = (jnp.abs(rel) <= WINDOW)
    valid = valid & ((ki >= tq) | (nb > 0)) & ((ki < 2 * tq) | (nb < blocks_per_seq - 1))
    dist = jnp.abs(rel).astype(F32)
    group = n_heads // n_kv_heads
    scale = head_dim ** -0.5
    for h in range(n_heads):
        j = h // group
        slope = 2.0 ** (-8.0 * (h + 1) / n_heads)
        qh = q_ref[:, h * head_dim:(h + 1) * head_dim]
        kj = k[:, j * head_dim:(j + 1) * head_dim]
        s = lax.dot_general(qh, kj, (((1,), (1,)), ((), ())), preferred_element_type=F32)
        s = jnp.where(valid, s * scale - slope * dist, -jnp.inf)
        sink = sink_ref[h]
        m = jnp.maximum(jnp.max(s, axis=-1, keepdims=True), sink)
        e = jnp.exp(s - m)
        denom = jnp.sum(e, axis=-1, keepdims=True) + jnp.exp(sink - m)
        pv = jnp.dot(e.astype(BF16), v[:, j * head_dim:(j + 1) * head_dim], preferred_element_type=F32)
        o_ref[:, h * head_dim:(h + 1) * head_dim] = (pv * (1.0 / denom)).astype(o_ref.dtype)


def _window_attention(proj, sink, *, seq, attn_width, kv_width, k_off, v_off):
    n = proj.shape[0]
    n_heads = sink.shape[0]
    head_dim = attn_width // n_heads
    n_kv_heads = kv_width // head_dim
    tq = Q_BLOCK
    nblk = n // tq
    kb, vb = k_off // kv_width, v_off // kv_width
    body = functools.partial(_attn_body, blocks_per_seq=seq // tq, n_heads=n_heads,
                             n_kv_heads=n_kv_heads, head_dim=head_dim)

    def band(col):
        return [
            pl.BlockSpec((tq, kv_width), lambda i: (jnp.maximum(i - 1, 0), col)),
            pl.BlockSpec((tq, kv_width), lambda i: (i, col)),
            pl.BlockSpec((tq, kv_width), lambda i: (jnp.minimum(i + 1, nblk - 1), col)),
        ]

    return pl.pallas_call(
        body,
        out_shape=jax.ShapeDtypeStruct((n, attn_width), BF16),
        grid=(nblk,),
        in_specs=[pl.BlockSpec(memory_space=pltpu.SMEM),
                  pl.BlockSpec((tq, attn_width), lambda i: (i, 0))] + band(kb) + band(vb),
        out_specs=pl.BlockSpec((tq, attn_width), lambda i: (i, 0)),
        compiler_params=_compiler_params(("parallel",), 4),
        name="window_attention",
    )(sink, proj, proj, proj, proj, proj, proj, proj)


CONV_ROWS = 64
CONV_LANES = 256


def _conv_body(*refs, tiles_per_seq, tt, c, cb, ksize):
    pieces = c // cb
    a_refs, b_refs = refs[:3 * pieces], refs[3 * pieces:6 * pieces]
    w_ref, bias_ref, lg_ref, lb_ref, o_ref, u_ref, shift_ref, y_ref = refs[6 * pieces:]
    ti = pl.program_id(0) % tiles_per_seq
    halo = V7X_BF16_SUBLANES

    def glu(a_ref, b_ref):
        return a_ref[...].astype(F32) * _sigmoid(b_ref[...].astype(F32))

    for p in range(pieces):
        ap_ref, ac_ref, an_ref = a_refs[3 * p:3 * p + 3]
        bp_ref, bc_ref, bn_ref = b_refs[3 * p:3 * p + 3]
        cols = slice(p * cb, (p + 1) * cb)
        u_ref[0:halo, cols] = jnp.where(ti > 0, glu(ap_ref, bp_ref), 0.0)
        u_ref[halo:halo + tt, cols] = glu(ac_ref, bc_ref)
        u_ref[halo + tt:, cols] = jnp.where(ti < tiles_per_seq - 1, glu(an_ref, bn_ref), 0.0)

    first = halo - ksize // 2
    n_rows = tt + 2 * halo
    cl, rr = min(c, CONV_LANES), min(tt, CONV_ROWS)
    for c0 in range(0, c, cl):
        lanes = slice(c0, c0 + cl)
        tile = u_ref[:, lanes]
        for phase in range(1, 8):
            shift_ref[phase - 1, :, lanes] = pltpu.roll(tile, n_rows - phase, 0)
        for r0 in range(0, tt, rr):
            acc = jnp.zeros((rr, cl), F32)
            for j in range(ksize):
                phase, a = (j + first) % 8, r0 + (j + first) // 8 * 8
                assert a + rr <= n_rows - 8
                src = u_ref[a:a + rr, lanes] if phase == 0 else shift_ref[phase - 1, a:a + rr, lanes]
                acc = acc + src * w_ref[j:j + 1, lanes]
            y_ref[r0:r0 + rr, lanes] = acc + bias_ref[:, lanes]

    y = y_ref[...]
    mu = jnp.mean(y, axis=-1, keepdims=True)
    yc = y - mu
    var = jnp.mean(yc * yc, axis=-1, keepdims=True)
    z = yc * lax.rsqrt(var + EPS) * lg_ref[...] + lb_ref[...]
    o_ref[...] = (z * _sigmoid(z)).astype(o_ref.dtype)


def _conformer_conv(proj, w_dw, b_dw, ln_g, ln_b, *, seq, a_off, b_off):
    n = proj.shape[0]
    ksize, c = w_dw.shape
    halo = V7X_BF16_SUBLANES
    assert ksize // 2 <= halo
    tt = _tile(seq, 128)
    cb = c
    while a_off % cb or b_off % cb:
        cb //= 2
    assert cb % 128 == 0 and c % cb == 0
    pieces = c // cb
    hb = tt // halo
    nhalo = n // halo
    body = functools.partial(_conv_body, tiles_per_seq=seq // tt, tt=tt, c=c, cb=cb, ksize=ksize)

    def band(col):
        return [
            pl.BlockSpec((halo, cb), lambda i: (jnp.maximum(i * hb - 1, 0), col)),
            pl.BlockSpec((tt, cb), lambda i: (i, col)),
            pl.BlockSpec((halo, cb), lambda i: (jnp.minimum((i + 1) * hb, nhalo - 1), col)),
        ]

    bands = []
    for off in (a_off, b_off):
        for p in range(pieces):
            bands += band(off // cb + p)
    row = lambda a: a.reshape(1, c)
    vec = pl.BlockSpec((1, c), lambda i: (0, 0))
    return pl.pallas_call(
        body,
        out_shape=jax.ShapeDtypeStruct((n, c), BF16),
        grid=(n // tt,),
        in_specs=bands + [pl.BlockSpec((ksize, c), lambda i: (0, 0)), vec, vec, vec],
        out_specs=pl.BlockSpec((tt, c), lambda i: (i, 0)),
        scratch_shapes=[pltpu.VMEM((tt + 2 * halo, c), F32), pltpu.VMEM((7, tt + 2 * halo, c), F32),
                        pltpu.VMEM((tt, c), F32)],
        compiler_params=_compiler_params(("parallel",), (8 * (tt + 2 * halo) + 5 * tt) * c * 4 / MIB),
        name="conformer_conv",
    )(*([proj] * (6 * pieces)), w_dw, row(b_dw), row(ln_g), row(ln_b))


def _merge_body(attn_ref, conv_ref, woa_ref, woc_ref, ga_ref, gc_ref, o_ref):
    ad = jnp.dot(attn_ref[...], woa_ref[...], preferred_element_type=F32)
    cd = jnp.dot(conv_ref[...], woc_ref[...], preferred_element_type=F32)
    merged = _sigmoid(ga_ref[...].astype(F32)) * ad + _sigmoid(gc_ref[...].astype(F32)) * cd
    o_ref[...] = merged.astype(o_ref.dtype)


def _gated_merge(attn, conv, w_oa, w_oc, proj, *, ga_off, gc_off):
    n, aw = attn.shape
    cw = conv.shape[1]
    d = w_oa.shape[1]
    tm, tn = _tile(n, 512), _tile(d, 1024)
    gab, gcb = ga_off // tn, gc_off // tn
    vmem = (2 * tm * (aw + cw) * 2 + 2 * (aw + cw) * tn * 2 + 6 * tm * tn * 2) / MIB
    return pl.pallas_call(
        _merge_body,
        out_shape=jax.ShapeDtypeStruct((n, d), BF16),
        grid=(n // tm, d // tn),
        in_specs=[
            pl.BlockSpec((tm, aw), lambda i, j: (i, 0)),
            pl.BlockSpec((tm, cw), lambda i, j: (i, 0)),
            pl.BlockSpec((aw, tn), lambda i, j: (0, j)),
            pl.BlockSpec((cw, tn), lambda i, j: (0, j)),
            pl.BlockSpec((tm, tn), lambda i, j: (i, gab + j)),
            pl.BlockSpec((tm, tn), lambda i, j: (i, gcb + j)),
        ],
        out_specs=pl.BlockSpec((tm, tn), lambda i, j: (i, j)),
        compiler_params=_compiler_params(("parallel", "parallel"), vmem),
        name="gated_merge",
    )(attn, conv, w_oa, w_oc, proj, proj)


def _outproj_body(m_ref, w_ref, x_ref, o_ref):
    o_ref[...] = x_ref[...] + jnp.dot(m_ref[...], w_ref[...], preferred_element_type=F32)


def _out_projection(merged, w_out, x2):
    n, d = merged.shape
    tm, tn = _tile(n, 512), _tile(d, 1024)
    vmem = (2 * tm * d * 2 + 2 * d * tn * 2 + 4 * tm * tn * 4) / MIB
    return pl.pallas_call(
        _outproj_body,
        out_shape=jax.ShapeDtypeStruct((n, d), F32),
        grid=(n // tm, d // tn),
        in_specs=[
            pl.BlockSpec((tm, d), lambda i, j: (i, 0)),
            pl.BlockSpec((d, tn), lambda i, j: (0, j)),
            pl.BlockSpec((tm, tn), lambda i, j: (i, j)),
        ],
        out_specs=pl.BlockSpec((tm, tn), lambda i, j: (i, j)),
        compiler_params=_compiler_params(("parallel", "parallel"), vmem),
        name="out_projection",
    )(merged, w_out, x2)


def _router_body(h_ref, g_ref, wrt_ref, rb_ref,
                 xp_ref, eidx_ref, rank_ref, wt_ref, cnt_ref, carry_ref, hn_ref, *, n_exp, tm):
    @pl.when(pl.program_id(0) == 0)
    def _():
        carry_ref[...] = jnp.zeros_like(carry_ref)

    def norm_rows(c, carry):
        rows = pl.ds(pl.multiple_of(c * NORM_ROWS, NORM_ROWS), NORM_ROWS)
        h = h_ref[rows, :]
        ms = jnp.mean(h * h, axis=-1, keepdims=True)
        hn = h * lax.rsqrt(ms + EPS) * g_ref[...]
        hn_ref[rows, :] = hn
        xp_ref[rows, :] = _pack_halves(hn.astype(BF16).astype(F32))
        return carry

    lax.fori_loop(0, tm // NORM_ROWS, norm_rows, 0)

    logits = lax.dot_general(wrt_ref[...], hn_ref[...], (((1,), (1,)), ((), ())),
                             precision=lax.Precision.HIGHEST, preferred_element_type=F32)
    score = _sigmoid(logits)
    choice = score + rb_ref[...]

    gsz = n_exp // N_GROUPS
    sub = lax.broadcasted_iota(I32, (gsz, tm), 0)
    groups, gscore = [], []
    for g in range(N_GROUPS):
        grp = choice[g * gsz:(g + 1) * gsz, :]
        m1 = jnp.max(grp, axis=0, keepdims=True)
        first = jnp.min(jnp.where(grp == m1, sub, gsz), axis=0, keepdims=True)
        m2 = jnp.max(jnp.where(sub == first, -jnp.inf, grp), axis=0, keepdims=True)
        groups.append(grp)
        gscore.append(m1 + m2)
    kept = []
    for g in range(N_GROUPS):
        beaten_by = jnp.zeros((1, tm), I32)
        for o in range(N_GROUPS):
            if o != g:
                wins = (gscore[o] >= gscore[g]) if o < g else (gscore[o] > gscore[g])
                beaten_by = beaten_by + wins.astype(I32)
        kept.append(jnp.where(beaten_by < TOPK_GROUPS, groups[g], -jnp.inf))
    cand = jnp.concatenate(kept, axis=0)

    eiota = lax.broadcasted_iota(I32, (n_exp, tm), 0)
    msel = jnp.zeros((n_exp, tm), F32)
    sel_idx, sel_score = [], []
    for _ in range(TOP_K):
        m = jnp.max(cand, axis=0, keepdims=True)
        first = jnp.min(jnp.where(cand == m, eiota, n_exp), axis=0, keepdims=True)
        hit = eiota == first
        sel_idx.append(first)
        sel_score.append(jnp.sum(jnp.where(hit, score, 0.0), axis=0, keepdims=True))
        msel = msel + hit.astype(F32)
        cand = jnp.where(hit, -jnp.inf, cand)

    rows = lax.broadcasted_iota(I32, (tm, tm), 0)
    cols = lax.broadcasted_iota(I32, (tm, tm), 1)
    earlier = (rows < cols).astype(BF16)
    before = jnp.dot(msel.astype(BF16), earlier, preferred_element_type=F32)
    rank_full = carry_ref[:, 0:1] + before
    total = carry_ref[...] + jnp.sum(msel, axis=1, keepdims=True)
    carry_ref[...] = total
    cnt_ref[...] = total

    wsum = sel_score[0]
    for k in range(1, TOP_K):
        wsum = wsum + sel_score[k]
    ranks = [jnp.sum(jnp.where(eiota == sel_idx[k], rank_full, 0.0), axis=0, keepdims=True)
             for k in range(TOP_K)]
    eidx_ref[...] = jnp.concatenate(sel_idx, axis=0)
    rank_ref[...] = jnp.concatenate(ranks, axis=0).astype(I32)
    wt_ref[...] = jnp.concatenate([sc / wsum * ROUTED_SCALE for sc in sel_score], axis=0)


def _const_spec(shape):
    return pl.BlockSpec(shape, lambda i: (0,) * len(shape), pipeline_mode=pl.Buffered(1))


def _router(h1, g, w_router_t, router_bias):
    n, d = h1.shape
    n_exp = w_router_t.shape[0]
    tm = _tile(n, 256)
    body = functools.partial(_router_body, n_exp=n_exp, tm=tm)
    vmem = (2 * tm * d * 4 + 2 * tm * d * 2 + n_exp * d * 4 + tm * d * 4) / MIB
    return pl.pallas_call(
        body,
        out_shape=(
            jax.ShapeDtypeStruct((n, d // 2), U32),
            jax.ShapeDtypeStruct((TOP_K, n), I32),
            jax.ShapeDtypeStruct((TOP_K, n), I32),
            jax.ShapeDtypeStruct((TOP_K, n), F32),
            jax.ShapeDtypeStruct((n_exp, 128), F32),
        ),
        grid=(n // tm,),
        in_specs=[
            pl.BlockSpec((tm, d), lambda i: (i, 0)),
            _const_spec((1, d)),
            _const_spec((n_exp, d)),
            _const_spec((n_exp, 1)),
        ],
        out_specs=(
            pl.BlockSpec((tm, d // 2), lambda i: (i, 0)),
            pl.BlockSpec((TOP_K, tm), lambda i: (0, i)),
            pl.BlockSpec((TOP_K, tm), lambda i: (0, i)),
            pl.BlockSpec((TOP_K, tm), lambda i: (0, i)),
            pl.BlockSpec((n_exp, 128), lambda i: (0, 0)),
        ),
        scratch_shapes=[pltpu.VMEM((n_exp, 128), F32), pltpu.VMEM((tm, d), F32)],
        compiler_params=_compiler_params(("arbitrary",), vmem),
        name="router",
    )(h1, g, w_router_t, router_bias)


def _swiglu(x_packed, wg, wu, wd):
    lo, hi = _unpack_halves(x_packed)
    x = jnp.concatenate([lo.astype(BF16), hi.astype(BF16)], axis=1)
    gate = jnp.dot(x, wg, preferred_element_type=F32)
    up = jnp.dot(x, wu, preferred_element_type=F32)
    act = (gate * _sigmoid(gate) * up).astype(BF16)
    return jnp.dot(act, wd, preferred_element_type=F32)


def _shared_body(xp_ref, h1_ref, wg_ref, wu_ref, wd_ref, h2_ref):
    h2_ref[...] = h1_ref[...] + _swiglu(xp_ref[...], wg_ref[...], wu_ref[...], wd_ref[...])


def _shared_expert(xp, h1, wg, wu, wd):
    n, d = h1.shape
    f = wg.shape[1]
    tm = _tile(n, 256)
    vmem = (2 * tm * d * 2 + 4 * tm * d * 4 + 3 * d * f * 2) / MIB
    return pl.pallas_call(
        _shared_body,
        out_shape=jax.ShapeDtypeStruct((n, d), F32),
        grid=(n // tm,),
        in_specs=[
            pl.BlockSpec((tm, d // 2), lambda i: (i, 0)),
            pl.BlockSpec((tm, d), lambda i: (i, 0)),
            _const_spec((d, f)),
            _const_spec((d, f)),
            _const_spec((f, d)),
        ],
        out_specs=pl.BlockSpec((tm, d), lambda i: (i, 0)),
        compiler_params=_compiler_params(("parallel",), vmem),
        name="shared_expert",
    )(xp, h1, wg, wu, wd)


def _dispatch_body(slot_ref, xp_ref, xs_ref, sem, *, tt):
    def row_copies(r):
        return [pltpu.make_async_copy(xp_ref.at[pl.ds(r, 1)],
                                      xs_ref.at[pl.ds(slot_ref[r * TOP_K + k], 1)], sem)
                for k in range(TOP_K)]

    def start(r, carry):
        for k, cp in enumerate(row_copies(r)):
            cp.start(priority=k % DMA_PRIORITIES)
        return carry

    def wait(r, carry):
        for cp in row_copies(r):
            cp.wait()
        return carry

    lax.fori_loop(0, tt, start, 0)
    lax.fori_loop(0, tt, wait, 0)


def _moe_dispatch(slot_flat, xp):
    n, half = xp.shape
    tt = _tile(n, 256)
    return pl.pallas_call(
        functools.partial(_dispatch_body, tt=tt),
        out_shape=jax.ShapeDtypeStruct((n * TOP_K, half), U32),
        grid=(n // tt,),
        in_specs=[
            pl.BlockSpec((tt * TOP_K,), lambda i: (i,), memory_space=pltpu.SMEM),
            pl.BlockSpec((tt, half), lambda i: (i, 0)),
        ],
        out_specs=pl.BlockSpec(memory_space=pl.ANY),
        scratch_shapes=[pltpu.SemaphoreType.DMA(())],
        compiler_params=_compiler_params(("arbitrary",), 2 * tt * half * 4 / MIB),
        name="moe_dispatch",
    )(slot_flat, xp)


def _ffn_body(vblk_ref, vexp_ref, vlo_ref, vhi_ref, xs_ref, wg_ref, wu_ref, wd_ref, ys_ref, *, rb):
    del vexp_ref
    v = pl.program_id(0)
    y = _swiglu(xs_ref[...], wg_ref[0], wu_ref[0], wd_ref[0])
    packed = _pack_halves(y.astype(BF16).astype(F32))
    first_visit = (v == 0) | (vblk_ref[v] != vblk_ref[jnp.maximum(v - 1, 0)])

    @pl.when(first_visit)
    def _():
        ys_ref[...] = packed

    @pl.when(jnp.logical_not(first_visit))
    def _():
        rows = vblk_ref[v] * rb + lax.broadcasted_iota(I32, packed.shape, 0)
        mine = (rows >= vlo_ref[v]) & (rows < vhi_ref[v])
        ys_ref[...] = jnp.where(mine, packed, ys_ref[...])


def _expert_ffn(visits, xs, wg, wu, wd, *, rb):
    rows, half = xs.shape
    n_exp, d, f = wg.shape
    n_visits = visits[0].shape[0]
    vmem = (2 * 3 * d * f * 2 + 4 * rb * half * 4) / MIB
    grid_spec = pltpu.PrefetchScalarGridSpec(
        num_scalar_prefetch=4,
        grid=(n_visits,),
        in_specs=[
            pl.BlockSpec((rb, half), lambda v, vb, ve, lo, hi: (vb[v], 0)),
            pl.BlockSpec((1, d, f), lambda v, vb, ve, lo, hi: (ve[v], 0, 0)),
            pl.BlockSpec((1, d, f), lambda v, vb, ve, lo, hi: (ve[v], 0, 0)),
            pl.BlockSpec((1, f, d), lambda v, vb, ve, lo, hi: (ve[v], 0, 0)),
        ],
        out_specs=pl.BlockSpec((rb, half), lambda v, vb, ve, lo, hi: (vb[v], 0)),
    )
    return pl.pallas_call(
        functools.partial(_ffn_body, rb=rb),
        out_shape=jax.ShapeDtypeStruct((rows, half), U32),
        grid_spec=grid_spec,
        compiler_params=_compiler_params(("arbitrary",), vmem),
        name="expert_ffn",
    )(*visits, xs, wg, wu, wd)


def _visit_tables(counts, n_rows, rb):
    n_exp = counts.shape[0]
    n_blocks = n_rows // rb
    n_visits = n_blocks + n_exp
    ends = jnp.cumsum(counts)
    starts = ends - counts
    first_blk = starts // rb
    last_blk = jnp.maximum(ends - 1, starts) // rb
    per_exp = jnp.where(counts > 0, last_blk - first_blk + 1, 0)
    vis_end = jnp.cumsum(per_exp)
    vis_start = vis_end - per_exp
    total = vis_end[-1]
    v = jnp.arange(n_visits, dtype=I32)
    e = jnp.minimum(jnp.sum((v[:, None] >= vis_end[None, :]).astype(I32), axis=1), n_exp - 1)
    is_e = e[:, None] == jnp.arange(n_exp, dtype=I32)[None, :]
    of_e = lambda table: jnp.sum(jnp.where(is_e, table[None, :], 0), axis=1)
    blk = of_e(first_blk) + (v - of_e(vis_start))
    lo = jnp.maximum(of_e(starts), blk * rb)
    hi = jnp.minimum(of_e(ends), (blk + 1) * rb)
    used = v < total
    last_e = jnp.max(jnp.where(counts > 0, jnp.arange(n_exp, dtype=I32), 0))
    blk = jnp.where(used, blk, n_blocks - 1).astype(I32)
    e = jnp.where(used, e, last_e).astype(I32)
    lo = jnp.where(used, lo, 0).astype(I32)
    hi = jnp.where(used, hi, 0).astype(I32)
    return blk, e, lo, hi


def _combine_body(slot_ref, next_slot_ref, w_ref, h2_ref, ys_ref, g_ref, o_ref, ybuf, sems, *, tt, d, tiles):
    half = d // 2
    i = pl.program_id(0)
    cur = i % 2

    def row_copies(slots, buf, r):
        return [pltpu.make_async_copy(ys_ref.at[pl.ds(slots[r * TOP_K + k], 1)],
                                      ybuf.at[buf, k, pl.ds(r, 1)], sems.at[buf])
                for k in range(TOP_K)]

    def start_tile(slots, buf):
        def start(r, carry):
            for k, cp in enumerate(row_copies(slots, buf, r)):
                cp.start(priority=k % DMA_PRIORITIES)
            return carry
        lax.fori_loop(0, tt, start, 0)

    @pl.when(i == 0)
    def _():
        start_tile(slot_ref, cur)

    @pl.when(i + 1 < tiles)
    def _():
        start_tile(next_slot_ref, 1 - cur)

    def wait(r, carry):
        for cp in row_copies(slot_ref, cur, r):
            cp.wait()
        return carry

    lax.fori_loop(0, tt, wait, 0)

    acc_lo = h2_ref[:, :half]
    acc_hi = h2_ref[:, half:]
    for k in range(TOP_K):
        lo, hi = _unpack_halves(ybuf[cur, k])
        wk = w_ref[:, k:k + 1]
        acc_lo = acc_lo + wk * lo
        acc_hi = acc_hi + wk * hi
    ss = jnp.sum(acc_lo * acc_lo, axis=-1, keepdims=True) + jnp.sum(acc_hi * acc_hi, axis=-1, keepdims=True)
    inv = lax.rsqrt(ss / d + EPS)
    o_ref[:, :half] = acc_lo * inv * g_ref[:, :half]
    o_ref[:, half:] = acc_hi * inv * g_ref[:, half:]


def _moe_combine(slot_flat, w_tok, h2, ys, g):
    n, d = h2.shape
    half = d // 2
    tt = _tile(n, 128)
    tiles = n // tt
    vmem = (2 * TOP_K * tt * half * 4 + 4 * tt * d * 4) / MIB
    return pl.pallas_call(
        functools.partial(_combine_body, tt=tt, d=d, tiles=tiles),
        out_shape=jax.ShapeDtypeStruct((n, d), F32),
        grid=(tiles,),
        in_specs=[
            pl.BlockSpec((tt * TOP_K,), lambda i: (i,), memory_space=pltpu.SMEM),
            pl.BlockSpec((tt * TOP_K,), lambda i: (jnp.minimum(i + 1, tiles - 1),), memory_space=pltpu.SMEM),
            pl.BlockSpec((tt, TOP_K), lambda i: (i, 0)),
            pl.BlockSpec((tt, d), lambda i: (i, 0)),
            pl.BlockSpec(memory_space=pl.ANY),
            pl.BlockSpec((1, d), lambda i: (0, 0)),
        ],
        out_specs=pl.BlockSpec((tt, d), lambda i: (i, 0)),
        scratch_shapes=[pltpu.VMEM((2, TOP_K, tt, half), U32), pltpu.SemaphoreType.DMA((2,))],
        compiler_params=_compiler_params(("arbitrary",), vmem),
        name="moe_combine",
    )(slot_flat, slot_flat, w_tok, h2, ys, g)


def _layer(h, l, p, dims):
    b, s, d = h.shape
    n = b * s
    x2 = h.reshape(n, d)
    aw, kvw, cw = dims["attn"], dims["kv"], dims["conv"]
    _, k_off, v_off, cv_off, cg_off, ga_off, gc_off = dims["src_offsets"]

    proj = _in_projection(x2, p["attn_norm_g"][l].reshape(1, d), p["w_in"][l].astype(BF16))
    attn = _window_attention(proj, p["sink_logits"][l], seq=s, attn_width=aw, kv_width=kvw,
                             k_off=k_off, v_off=v_off)
    conv = _conformer_conv(proj, p["conv_dw_w"][l], p["conv_dw_b"][l], p["conv_ln_g"][l],
                           p["conv_ln_b"][l], seq=s, a_off=cv_off, b_off=cg_off)
    merged = _gated_merge(attn, conv, p["w_o_attn"][l].astype(BF16), p["w_o_conv"][l].astype(BF16),
                          proj, ga_off=ga_off, gc_off=gc_off)
    h1 = _out_projection(merged, p["w_out"][l].astype(BF16), x2)

    n_exp = p["w_router"].shape[-1]
    xp, eidx, rank, wts, cnt = _router(
        h1, p["ffn_norm_g"][l].reshape(1, d), p["w_router"][l].T, p["router_bias"][l].reshape(n_exp, 1))
    h2 = _shared_expert(xp, h1, p["w_sh_gate"][l].astype(BF16), p["w_sh_up"][l].astype(BF16),
                        p["w_sh_down"][l].astype(BF16))

    counts = cnt[:, 0].astype(I32)
    starts = jnp.cumsum(counts) - counts
    is_e = eidx[None, :, :] == jnp.arange(n_exp, dtype=I32)[:, None, None]
    slot = rank + jnp.sum(jnp.where(is_e, starts[:, None, None], 0), axis=0)
    slot_flat = slot.T.reshape(-1)
    rb = _tile(n * TOP_K, 256)
    visits = _visit_tables(counts, n * TOP_K, rb)

    xs = _moe_dispatch(slot_flat, xp)
    ys = _expert_ffn(visits, xs, p["w_exp_gate"][l].astype(BF16), p["w_exp_up"][l].astype(BF16),
                     p["w_exp_down"][l].astype(BF16), rb=rb)
    return slot_flat, wts.T, h2, ys


def kernel(x, attn_norm_g, w_in, sink_logits, w_o_attn, conv_dw_w, conv_dw_b, conv_ln_g, conv_ln_b,
           w_o_conv, w_out, ffn_norm_g, w_router, router_bias, w_exp_gate, w_exp_up, w_exp_down,
           w_sh_gate, w_sh_up, w_sh_down, final_norm_g):
    b, s, d = x.shape
    depth = w_in.shape[0]
    assert depth == 1, "the final RMSNorm is fused into the only layer's MoE combine"
    aw = w_o_attn.shape[1]
    cw = conv_dw_w.shape[2]
    kvw = (w_in.shape[2] - aw - 2 * cw - 2 * d) // 2
    assert s % Q_BLOCK == 0 and WINDOW <= Q_BLOCK
    src = [0, aw, aw + kvw, aw + 2 * kvw, aw + 2 * kvw + cw, aw + 2 * kvw + 2 * cw, aw + 2 * kvw + 2 * cw + d]
    dims = {"attn": aw, "kv": kvw, "conv": cw, "src_offsets": src}
    p = dict(attn_norm_g=attn_norm_g, w_in=w_in, sink_logits=sink_logits, w_o_attn=w_o_attn,
             conv_dw_w=conv_dw_w, conv_dw_b=conv_dw_b, conv_ln_g=conv_ln_g, conv_ln_b=conv_ln_b,
             w_o_conv=w_o_conv, w_out=w_out, ffn_norm_g=ffn_norm_g, w_router=w_router,
             router_bias=router_bias, w_exp_gate=w_exp_gate, w_exp_up=w_exp_up, w_exp_down=w_exp_down,
             w_sh_gate=w_sh_gate, w_sh_up=w_sh_up, w_sh_down=w_sh_down)
    slot_flat, w_tok, h2, ys = _layer(x, 0, p, dims)
    out = _moe_combine(slot_flat, w_tok, h2, ys, final_norm_g.reshape(1, d))
    return out.reshape(b, s, d)
```

```python
import functools

import jax
import jax.numpy as jnp
from jax import lax
from jax.experimental import pallas as pl
from jax.experimental.pallas import tpu as pltpu

F32, BF16, U32, I32 = jnp.float32, jnp.bfloat16, jnp.uint32, jnp.int32

EPS = 1e-6
WINDOW = 128
Q_BLOCK = 128
TOP_K = 8
N_GROUPS = 8
TOPK_GROUPS = 4
ROUTED_SCALE = 2.5

V7X_VMEM_BYTES = 64 * 1024 * 1024
V7X_BF16_SUBLANES = 16
MIB = 1024 * 1024
VMEM_CEILING_BYTES = V7X_VMEM_BYTES - 6 * MIB
VMEM_TEMPORARIES_MIB = 16
NORM_ROWS = 32
DMA_PRIORITIES = 2


def _compiler_params(semantics, window_mib):
    return pltpu.CompilerParams(
        dimension_semantics=semantics,
        vmem_limit_bytes=min(int((window_mib + VMEM_TEMPORARIES_MIB) * MIB), VMEM_CEILING_BYTES),
    )


def _tile(dim, target):
    t = min(dim, target)
    while dim % t:
        t //= 2
    return t


def _sigmoid(v):
    return 1.0 / (1.0 + jnp.exp(-v))


def _pack_halves(v_f32):
    bits = lax.bitcast_convert_type(v_f32, U32)
    h = v_f32.shape[1] // 2
    return (bits[:, :h] >> 16) | bits[:, h:]


def _unpack_halves(p_u32):
    lo = lax.bitcast_convert_type(p_u32 << 16, F32)
    hi = lax.bitcast_convert_type(p_u32 & jnp.uint32(0xFFFF0000), F32)
    return lo, hi


def _inproj_body(x_ref, g_ref, w_ref, o_ref, xn_ref):
    @pl.when(pl.program_id(1) == 0)
    def _():
        def norm_rows(c, carry):
            rows = pl.ds(pl.multiple_of(c * NORM_ROWS, NORM_ROWS), NORM_ROWS)
            x = x_ref[rows, :]
            ms = jnp.mean(x * x, axis=-1, keepdims=True)
            xn_ref[rows, :] = (x * lax.rsqrt(ms + EPS) * g_ref[...]).astype(BF16)
            return carry

        lax.fori_loop(0, x_ref.shape[0] // NORM_ROWS, norm_rows, 0)

    o_ref[...] = jnp.dot(xn_ref[...], w_ref[...], preferred_element_type=F32).astype(o_ref.dtype)


def _in_projection(x2, g, w):
    n, d = x2.shape
    wid = w.shape[1]
    tm, tn = _tile(n, 512), _tile(wid, 1024)
    vmem = (2 * tm * d * 4 + tm * d * 2 + 2 * d * tn * 2 + 2 * tm * tn * 2) / MIB
    return pl.pallas_call(
        _inproj_body,
        out_shape=jax.ShapeDtypeStruct((n, wid), BF16),
        grid=(n // tm, wid // tn),
        in_specs=[
            pl.BlockSpec((tm, d), lambda i, j: (i, 0)),
            pl.BlockSpec((1, d), lambda i, j: (0, 0)),
            pl.BlockSpec((d, tn), lambda i, j: (0, j)),
        ],
        out_specs=pl.BlockSpec((tm, tn), lambda i, j: (i, j)),
        scratch_shapes=[pltpu.VMEM((tm, d), BF16)],
        compiler_params=_compiler_params(("parallel", "arbitrary"), vmem),
        name="in_projection",
    )(x2, g, w)


def _attn_body(sink_ref, q_ref, kp_ref, kc_ref, kn_ref, vp_ref, vc_ref, vn_ref, o_ref, *,
               blocks_per_seq, n_heads, n_kv_heads, head_dim):
    tq = Q_BLOCK
    nb = pl.program_id(0) % blocks_per_seq
    k = jnp.concatenate([kp_ref[...], kc_ref[...], kn_ref[...]], axis=0)
    v = jnp.concatenate([vp_ref[...], vc_ref[...], vn_ref[...]], axis=0)
    qi = lax.broadcasted_iota(I32, (tq, 3 * tq), 0)
    ki = lax.broadcasted_iota(I32, (tq, 3 * tq), 1)
    rel = ki - tq - qi
    valid = (jnp.abs(rel) <= WINDOW)
    valid = valid & ((ki >= tq) | (nb > 0)) & ((ki < 2 * tq) | (nb < blocks_per_seq - 1))
    dist = jnp.abs(rel).astype(F32)
    group = n_heads // n_kv_heads
    scale = head_dim ** -0.5
    for h in range(n_heads):
        j = h // group
        slope = 2.0 ** (-8.0 * (h + 1) / n_heads)
        qh = q_ref[:, h * head_dim:(h + 1) * head_dim]
        kj = k[:, j * head_dim:(j + 1) * head_dim]
        s = lax.dot_general(qh, kj, (((1,), (1,)), ((), ())), preferred_element_type=F32)
        s = jnp.where(valid, s * scale - slope * dist, -jnp.inf)
        sink = sink_ref[h]
        m = jnp.maximum(jnp.max(s, axis=-1, keepdims=True), sink)
        e = jnp.exp(s - m)
        denom = jnp.sum(e, axis=-1, keepdims=True) + jnp.exp(sink - m)
        pv = jnp.dot(e.astype(BF16), v[:, j * head_dim:(j + 1) * head_dim], preferred_element_type=F32)
        o_ref[:, h * head_dim:(h + 1) * head_dim] = (pv * (1.0 / denom)).astype(o_ref.dtype)


def _window_attention(proj, sink, *, seq, attn_width, kv_width, k_off, v_off):
    n = proj.shape[0]
    n_heads = sink.shape[0]
    head_dim = attn_width // n_heads
    n_kv_heads = kv_width // head_dim
    tq = Q_BLOCK
    nblk = n // tq
    kb, vb = k_off // kv_width, v_off // kv_width
    body = functools.partial(_attn_body, blocks_per_seq=seq // tq, n_heads=n_heads,
                             n_kv_heads=n_kv_heads, head_dim=head_dim)

    def band(col):
        return [
            pl.BlockSpec((tq, kv_width), lambda i: (jnp.maximum(i - 1, 0), col)),
            pl.BlockSpec((tq, kv_width), lambda i: (i, col)),
            pl.BlockSpec((tq, kv_width), lambda i: (jnp.minimum(i + 1, nblk - 1), col)),
        ]

    return pl.pallas_call(
        body,
        out_shape=jax.ShapeDtypeStruct((n, attn_width), BF16),
        grid=(nblk,),
        in_specs=[pl.BlockSpec(memory_space=pltpu.SMEM),
                  pl.BlockSpec((tq, attn_width), lambda i: (i, 0))] + band(kb) + band(vb),
        out_specs=pl.BlockSpec((tq, attn_width), lambda i: (i, 0)),
        compiler_params=_compiler_params(("parallel",), 4),
        name="window_attention",
    )(sink, proj, proj, proj, proj, proj, proj, proj)


CONV_ROWS = 64
CONV_LANES = 256


def _conv_body(*refs, tiles_per_seq, tt, c, cb, ksize):
    pieces = c // cb
    a_refs, b_refs = refs[:3 * pieces], refs[3 * pieces:6 * pieces]
    w_ref, bias_ref, lg_ref, lb_ref, o_ref, u_ref, shift_ref, y_ref = refs[6 * pieces:]
    ti = pl.program_id(0) % tiles_per_seq
    halo = V7X_BF16_SUBLANES

    def glu(a_ref, b_ref):
        return a_ref[...].astype(F32) * _sigmoid(b_ref[...].astype(F32))

    for p in range(pieces):
        ap_ref, ac_ref, an_ref = a_refs[3 * p:3 * p + 3]
        bp_ref, bc_ref, bn_ref = b_refs[3 * p:3 * p + 3]
        cols = slice(p * cb, (p + 1) * cb)
        u_ref[0:halo, cols] = jnp.where(ti > 0, glu(ap_ref, bp_ref), 0.0)
        u_ref[halo:halo + tt, cols] = glu(ac_ref, bc_ref)
        u_ref[halo + tt:, cols] = jnp.where(ti < tiles_per_seq - 1, glu(an_ref, bn_ref), 0.0)

    first = halo - ksize // 2
    n_rows = tt + 2 * halo
    cl, rr = min(c, CONV_LANES), min(tt, CONV_ROWS)
    for c0 in range(0, c, cl):
        lanes = slice(c0, c0 + cl)
        tile = u_ref[:, lanes]
        for phase in range(1, 8):
            shift_ref[phase - 1, :, lanes] = pltpu.roll(tile, n_rows - phase, 0)
        for r0 in range(0, tt, rr):
            acc = jnp.zeros((rr, cl), F32)
            for j in range(ksize):
                phase, a = (j + first) % 8, r0 + (j + first) // 8 * 8
                assert a + rr <= n_rows - 8
                src = u_ref[a:a + rr, lanes] if phase == 0 else shift_ref[phase - 1, a:a + rr, lanes]
                acc = acc + src * w_ref[j:j + 1, lanes]
            y_ref[r0:r0 + rr, lanes] = acc + bias_ref[:, lanes]

    y = y_ref[...]
    mu = jnp.mean(y, axis=-1, keepdims=True)
    yc = y - mu
    var = jnp.mean(yc * yc, axis=-1, keepdims=True)
    z = yc * lax.rsqrt(var + EPS) * lg_ref[...] + lb_ref[...]
    o_ref[...] = (z * _sigmoid(z)).astype(o_ref.dtype)


def _conformer_conv(proj, w_dw, b_dw, ln_g, ln_b, *, seq, a_off, b_off):
    n = proj.shape[0]
    ksize, c = w_dw.shape
    halo = V7X_BF16_SUBLANES
    assert ksize // 2 <= halo
    tt = _tile(seq, 128)
    cb = c
    while a_off % cb or b_off % cb:
        cb //= 2
    assert cb % 128 == 0 and c % cb == 0
    pieces = c // cb
    hb = tt // halo
    nhalo = n // halo
    body = functools.partial(_conv_body, tiles_per_seq=seq // tt, tt=tt, c=c, cb=cb, ksize=ksize)

    def band(col):
        return [
            pl.BlockSpec((halo, cb), lambda i: (jnp.maximum(i * hb - 1, 0), col)),
            pl.BlockSpec((tt, cb), lambda i: (i, col)),
            pl.BlockSpec((halo, cb), lambda i: (jnp.minimum((i + 1) * hb, nhalo - 1), col)),
        ]

    bands = []
    for off in (a_off, b_off):
        for p in range(pieces):
            bands += band(off // cb + p)
    row = lambda a: a.reshape(1, c)
    vec = pl.BlockSpec((1, c), lambda i: (0, 0))
    return pl.pallas_call(
        body,
        out_shape=jax.ShapeDtypeStruct((n, c), BF16),
        grid=(n // tt,),
        in_specs=bands + [pl.BlockSpec((ksize, c), lambda i: (0, 0)), vec, vec, vec],
        out_specs=pl.BlockSpec((tt, c), lambda i: (i, 0)),
        scratch_shapes=[pltpu.VMEM((tt + 2 * halo, c), F32), pltpu.VMEM((7, tt + 2 * halo, c), F32),
                        pltpu.VMEM((tt, c), F32)],
        compiler_params=_compiler_params(("parallel",), (8 * (tt + 2 * halo) + 5 * tt) * c * 4 / MIB),
        name="conformer_conv",
    )(*([proj] * (6 * pieces)), w_dw, row(b_dw), row(ln_g), row(ln_b))


def _merge_body(attn_ref, conv_ref, woa_ref, woc_ref, ga_ref, gc_ref, o_ref):
    ad = jnp.dot(attn_ref[...], woa_ref[...], preferred_element_type=F32)
    cd = jnp.dot(conv_ref[...], woc_ref[...], preferred_element_type=F32)
    merged = _sigmoid(ga_ref[...].astype(F32)) * ad + _sigmoid(gc_ref[...].astype(F32)) * cd
    o_ref[...] = merged.astype(o_ref.dtype)


def _gated_merge(attn, conv, w_oa, w_oc, proj, *, ga_off, gc_off):
    n, aw = attn.shape
    cw = conv.shape[1]
    d = w_oa.shape[1]
    tm, tn = _tile(n, 512), _tile(d, 1024)
    gab, gcb = ga_off // tn, gc_off // tn
    vmem = (2 * tm * (aw + cw) * 2 + 2 * (aw + cw) * tn * 2 + 6 * tm * tn * 2) / MIB
    return pl.pallas_call(
        _merge_body,
        out_shape=jax.ShapeDtypeStruct((n, d), BF16),
        grid=(n // tm, d // tn),
        in_specs=[
            pl.BlockSpec((tm, aw), lambda i, j: (i, 0)),
            pl.BlockSpec((tm, cw), lambda i, j: (i, 0)),
            pl.BlockSpec((aw, tn), lambda i, j: (0, j)),
            pl.BlockSpec((cw, tn), lambda i, j: (0, j)),
            pl.BlockSpec((tm, tn), lambda i, j: (i, gab + j)),
            pl.BlockSpec((tm, tn), lambda i, j: (i, gcb + j)),
        ],
        out_specs=pl.BlockSpec((tm, tn), lambda i, j: (i, j)),
        compiler_params=_compiler_params(("parallel", "parallel"), vmem),
        name="gated_merge",
    )(attn, conv, w_oa, w_oc, proj, proj)


def _outproj_body(m_ref, w_ref, x_ref, o_ref):
    o_ref[...] = x_ref[...] + jnp.dot(m_ref[...], w_ref[...], preferred_element_type=F32)


def _out_projection(merged, w_out, x2):
    n, d = merged.shape
    tm, tn = _tile(n, 512), _tile(d, 1024)
    vmem = (2 * tm * d * 2 + 2 * d * tn * 2 + 4 * tm * tn * 4) / MIB
    return pl.pallas_call(
        _outproj_body,
        out_shape=jax.ShapeDtypeStruct((n, d), F32),
        grid=(n // tm, d // tn),
        in_specs=[
            pl.BlockSpec((tm, d), lambda i, j: (i, 0)),
            pl.BlockSpec((d, tn), lambda i, j: (0, j)),
            pl.BlockSpec((tm, tn), lambda i, j: (i, j)),
        ],
        out_specs=pl.BlockSpec((tm, tn), lambda i, j: (i, j)),
        compiler_params=_compiler_params(("parallel", "parallel"), vmem),
        name="out_projection",
    )(merged, w_out, x2)


def _router_body(h_ref, g_ref, wrt_ref, rb_ref,
                 xp_ref, eidx_ref, rank_ref, wt_ref, cnt_ref, carry_ref, hn_ref, *, n_exp, tm):
    @pl.when(pl.program_id(0) == 0)
    def _():
        carry_ref[...] = jnp.zeros_like(carry_ref)

    def norm_rows(c, carry):
        rows = pl.ds(pl.multiple_of(c * NORM_ROWS, NORM_ROWS), NORM_ROWS)
        h = h_ref[rows, :]
        ms = jnp.mean(h * h, axis=-1, keepdims=True)
        hn = h * lax.rsqrt(ms + EPS) * g_ref[...]
        hn_ref[rows, :] = hn
        xp_ref[rows, :] = _pack_halves(hn.astype(BF16).astype(F32))
        return carry

    lax.fori_loop(0, tm // NORM_ROWS, norm_rows, 0)

    logits = lax.dot_general(wrt_ref[...], hn_ref[...], (((1,), (1,)), ((), ())),
                             precision=lax.Precision.HIGHEST, preferred_element_type=F32)
    score = _sigmoid(logits)
    choice = score + rb_ref[...]

    gsz = n_exp // N_GROUPS
    sub = lax.broadcasted_iota(I32, (gsz, tm), 0)
    groups, gscore = [], []
    for g in range(N_GROUPS):
        grp = choice[g * gsz:(g + 1) * gsz, :]
        m1 = jnp.max(grp, axis=0, keepdims=True)
        first = jnp.min(jnp.where(grp == m1, sub, gsz), axis=0, keepdims=True)
        m2 = jnp.max(jnp.where(sub == first, -jnp.inf, grp), axis=0, keepdims=True)
        groups.append(grp)
        gscore.append(m1 + m2)
    kept = []
    for g in range(N_GROUPS):
        beaten_by = jnp.zeros((1, tm), I32)
        for o in range(N_GROUPS):
            if o != g:
                wins = (gscore[o] >= gscore[g]) if o < g else (gscore[o] > gscore[g])
                beaten_by = beaten_by + wins.astype(I32)
        kept.append(jnp.where(beaten_by < TOPK_GROUPS, groups[g], -jnp.inf))
    cand = jnp.concatenate(kept, axis=0)

    eiota = lax.broadcasted_iota(I32, (n_exp, tm), 0)
    msel = jnp.zeros((n_exp, tm), F32)
    sel_idx, sel_score = [], []
    for _ in range(TOP_K):
        m = jnp.max(cand, axis=0, keepdims=True)
        first = jnp.min(jnp.where(cand == m, eiota, n_exp), axis=0, keepdims=True)
        hit = eiota == first
        sel_idx.append(first)
        sel_score.append(jnp.sum(jnp.where(hit, score, 0.0), axis=0, keepdims=True))
        msel = msel + hit.astype(F32)
        cand = jnp.where(hit, -jnp.inf, cand)

    rows = lax.broadcasted_iota(I32, (tm, tm), 0)
    cols = lax.broadcasted_iota(I32, (tm, tm), 1)
    earlier = (rows < cols).astype(BF16)
    before = jnp.dot(msel.astype(BF16), earlier, preferred_element_type=F32)
    rank_full = carry_ref[:, 0:1] + before
    total = carry_ref[...] + jnp.sum(msel, axis=1, keepdims=True)
    carry_ref[...] = total
    cnt_ref[...] = total

    wsum = sel_score[0]
    for k in range(1, TOP_K):
        wsum = wsum + sel_score[k]
    ranks = [jnp.sum(jnp.where(eiota == sel_idx[k], rank_full, 0.0), axis=0, keepdims=True)
             for k in range(TOP_K)]
    eidx_ref[...] = jnp.concatenate(sel_idx, axis=0)
    rank_ref[...] = jnp.concatenate(ranks, axis=0).astype(I32)
    wt_ref[...] = jnp.concatenate([sc / wsum * ROUTED_SCALE for sc in sel_score], axis=0)


def _const_spec(shape):
    return pl.BlockSpec(shape, lambda i: (0,) * len(shape), pipeline_mode=pl.Buffered(1))


def _router(h1, g, w_router_t, router_bias):
    n, d = h1.shape
    n_exp = w_router_t.shape[0]
    tm = _tile(n, 256)
    body = functools.partial(_router_body, n_exp=n_exp, tm=tm)
    vmem = (2 * tm * d * 4 + 2 * tm * d * 2 + n_exp * d * 4 + tm * d * 4) / MIB
    return pl.pallas_call(
        body,
        out_shape=(
            jax.ShapeDtypeStruct((n, d // 2), U32),
            jax.ShapeDtypeStruct((TOP_K, n), I32),
            jax.ShapeDtypeStruct((TOP_K, n), I32),
            jax.ShapeDtypeStruct((TOP_K, n), F32),
            jax.ShapeDtypeStruct((n_exp, 128), F32),
        ),
        grid=(n // tm,),
        in_specs=[
            pl.BlockSpec((tm, d), lambda i: (i, 0)),
            _const_spec((1, d)),
            _const_spec((n_exp, d)),
            _const_spec((n_exp, 1)),
        ],
        out_specs=(
            pl.BlockSpec((tm, d // 2), lambda i: (i, 0)),
            pl.BlockSpec((TOP_K, tm), lambda i: (0, i)),
            pl.BlockSpec((TOP_K, tm), lambda i: (0, i)),
            pl.BlockSpec((TOP_K, tm), lambda i: (0, i)),
            pl.BlockSpec((n_exp, 128), lambda i: (0, 0)),
        ),
        scratch_shapes=[pltpu.VMEM((n_exp, 128), F32), pltpu.VMEM((tm, d), F32)],
        compiler_params=_compiler_params(("arbitrary",), vmem),
        name="router",
    )(h1, g, w_router_t, router_bias)


def _swiglu(x_packed, wg, wu, wd):
    lo, hi = _unpack_halves(x_packed)
    x = jnp.concatenate([lo.astype(BF16), hi.astype(BF16)], axis=1)
    gate = jnp.dot(x, wg, preferred_element_type=F32)
    up = jnp.dot(x, wu, preferred_element_type=F32)
    act = (gate * _sigmoid(gate) * up).astype(BF16)
    return jnp.dot(act, wd, preferred_element_type=F32)


def _shared_body(xp_ref, h1_ref, wg_ref, wu_ref, wd_ref, h2_ref):
    h2_ref[...] = h1_ref[...] + _swiglu(xp_ref[...], wg_ref[...], wu_ref[...], wd_ref[...])


def _shared_expert(xp, h1, wg, wu, wd):
    n, d = h1.shape
    f = wg.shape[1]
    tm = _tile(n, 256)
    vmem = (2 * tm * d * 2 + 4 * tm * d * 4 + 3 * d * f * 2) / MIB
    return pl.pallas_call(
        _shared_body,
        out_shape=jax.ShapeDtypeStruct((n, d), F32),
        grid=(n // tm,),
        in_specs=[
            pl.BlockSpec((tm, d // 2), lambda i: (i, 0)),
            pl.BlockSpec((tm, d), lambda i: (i, 0)),
            _const_spec((d, f)),
            _const_spec((d, f)),
            _const_spec((f, d)),
        ],
        out_specs=pl.BlockSpec((tm, d), lambda i: (i, 0)),
        compiler_params=_compiler_params(("parallel",), vmem),
        name="shared_expert",
    )(xp, h1, wg, wu, wd)


def _dispatch_body(slot_ref, xp_ref, xs_ref, sem, *, tt):
    def row_copies(r):
        return [pltpu.make_async_copy(xp_ref.at[pl.ds(r, 1)],
                                      xs_ref.at[pl.ds(slot_ref[r * TOP_K + k], 1)], sem)
                for k in range(TOP_K)]

    def start(r, carry):
        for k, cp in enumerate(row_copies(r)):
            cp.start(priority=k % DMA_PRIORITIES)
        return carry

    def wait(r, carry):
        for cp in row_copies(r):
            cp.wait()
        return carry

    lax.fori_loop(0, tt, start, 0)
    lax.fori_loop(0, tt, wait, 0)


def _moe_dispatch(slot_flat, xp):
    n, half = xp.shape
    tt = _tile(n, 256)
    return pl.pallas_call(
        functools.partial(_dispatch_body, tt=tt),
        out_shape=jax.ShapeDtypeStruct((n * TOP_K, half), U32),
        grid=(n // tt,),
        in_specs=[
            pl.BlockSpec((tt * TOP_K,), lambda i: (i,), memory_space=pltpu.SMEM),
            pl.BlockSpec((tt, half), lambda i: (i, 0)),
        ],
        out_specs=pl.BlockSpec(memory_space=pl.ANY),
        scratch_shapes=[pltpu.SemaphoreType.DMA(())],
        compiler_params=_compiler_params(("arbitrary",), 2 * tt * half * 4 / MIB),
        name="moe_dispatch",
    )(slot_flat, xp)


def _ffn_body(vblk_ref, vexp_ref, vlo_ref, vhi_ref, vpar_ref, nexp_ref, clo_ref, chi_ref,
              xs_ref, wg_hbm, wu_hbm, wd_hbm, ys_ref,
              wg_ref, wu_ref, wd_ref, sg_ref, su_ref, sd_ref, sems, *, rb, chunks):
    v = pl.program_id(0)
    mats = ((wg_hbm, wg_ref, sg_ref), (wu_hbm, wu_ref, su_ref), (wd_hbm, wd_ref, sd_ref))

    def chunk_copy(m, e, c, stage):
        hbm, _, stage_ref = mats[m]
        rows = stage_ref.shape[1]
        return pltpu.make_async_copy(hbm.at[e, pl.ds(c * rows, rows)], stage_ref.at[stage], sems.at[m, stage])

    def start_chunks(e, lo, hi):
        for m in range(3):
            for ahead in range(2):
                @pl.when(lo + ahead < hi)
                def _():
                    chunk_copy(m, e, lo + ahead, ahead).start()

    def convert_chunks(e, par, lo, hi):
        def convert(c, carry):
            stage = (c - lo) % 2
            for m in range(3):
                _, dst_ref, stage_ref = mats[m]
                rows = stage_ref.shape[1]
                chunk_copy(m, e, c, stage).wait()
                dst_ref[par, pl.ds(pl.multiple_of(c * rows, rows), rows), :] = stage_ref[stage].astype(BF16)

                @pl.when(c + 2 < hi)
                def _():
                    chunk_copy(m, e, c + 2, stage).start()
            return carry

        lax.fori_loop(lo, hi, convert, 0)

    @pl.when(v == 0)
    def _():
        start_chunks(vexp_ref[0], 0, chunks)
        convert_chunks(vexp_ref[0], vpar_ref[0], 0, chunks)

    start_chunks(nexp_ref[v], clo_ref[v], chi_ref[v])
    par = vpar_ref[v]
    y = _swiglu(xs_ref[...], wg_ref[par], wu_ref[par], wd_ref[par])
    packed = _pack_halves(y.astype(BF16).astype(F32))
    first_visit = (v == 0) | (vblk_ref[v] != vblk_ref[jnp.maximum(v - 1, 0)])

    @pl.when(first_visit)
    def _():
        ys_ref[...] = packed

    @pl.when(jnp.logical_not(first_visit))
    def _():
        rows = vblk_ref[v] * rb + lax.broadcasted_iota(I32, packed.shape, 0)
        mine = (rows >= vlo_ref[v]) & (rows < vhi_ref[v])
        ys_ref[...] = jnp.where(mine, packed, ys_ref[...])

    convert_chunks(nexp_ref[v], 1 - par, clo_ref[v], chi_ref[v])


def _weight_chunks(f):
    chunks = 16
    while f % (chunks * V7X_BF16_SUBLANES):
        chunks //= 2
    return chunks


def _expert_ffn(visits, xs, wg, wu, wd, *, rb):
    rows, half = xs.shape
    n_exp, d, f = wg.shape
    n_visits = visits[0].shape[0]
    chunks = _weight_chunks(f)
    vmem = (2 * 3 * d * f * 2 + 3 * 2 * (d // chunks) * f * 4 + 4 * rb * half * 4) / MIB
    block = lambda v, *tables: (tables[0][v], 0)
    grid_spec = pltpu.PrefetchScalarGridSpec(
        num_scalar_prefetch=len(visits),
        grid=(n_visits,),
        in_specs=[
            pl.BlockSpec((rb, half), block),
            pl.BlockSpec(memory_space=pl.ANY),
            pl.BlockSpec(memory_space=pl.ANY),
            pl.BlockSpec(memory_space=pl.ANY),
        ],
        out_specs=pl.BlockSpec((rb, half), block),
        scratch_shapes=[
            pltpu.VMEM((2, d, f), BF16), pltpu.VMEM((2, d, f), BF16), pltpu.VMEM((2, f, d), BF16),
            pltpu.VMEM((2, d // chunks, f), F32), pltpu.VMEM((2, d // chunks, f), F32),
            pltpu.VMEM((2, f // chunks, d), F32),
            pltpu.SemaphoreType.DMA((3, 2)),
        ],
    )
    return pl.pallas_call(
        functools.partial(_ffn_body, rb=rb, chunks=chunks),
        out_shape=jax.ShapeDtypeStruct((rows, half), U32),
        grid_spec=grid_spec,
        compiler_params=_compiler_params(("arbitrary",), vmem),
        name="expert_ffn",
    )(*visits, xs, wg, wu, wd)


def _visit_tables(counts, n_rows, rb, chunks):
    n_exp = counts.shape[0]
    n_blocks = n_rows // rb
    n_visits = n_blocks + n_exp
    ends = jnp.cumsum(counts)
    starts = ends - counts
    first_blk = starts // rb
    last_blk = jnp.maximum(ends - 1, starts) // rb
    per_exp = jnp.where(counts > 0, last_blk - first_blk + 1, 0)
    vis_end = jnp.cumsum(per_exp)
    vis_start = vis_end - per_exp
    total = vis_end[-1]
    v = jnp.arange(n_visits, dtype=I32)
    e = jnp.minimum(jnp.sum((v[:, None] >= vis_end[None, :]).astype(I32), axis=1), n_exp - 1)
    is_e = e[:, None] == jnp.arange(n_exp, dtype=I32)[None, :]
    of_e = lambda table: jnp.sum(jnp.where(is_e, table[None, :], 0), axis=1)
    blk = of_e(first_blk) + (v - of_e(vis_start))
    lo = jnp.maximum(of_e(starts), blk * rb)
    hi = jnp.minimum(of_e(ends), (blk + 1) * rb)
    used = v < total
    ids = jnp.arange(n_exp, dtype=I32)
    nonempty = counts > 0
    last_e = jnp.max(jnp.where(nonempty, ids, 0))
    slot_of = (jnp.cumsum(nonempty.astype(I32)) - 1) % 2
    later = (ids[None, :] > ids[:, None]) & nonempty[None, :]
    next_of = jnp.min(jnp.where(later, ids[None, :], n_exp), axis=1)
    step_in_e = v - of_e(vis_start)
    steps_of_e = jnp.maximum(of_e(per_exp), 1)
    has_next = used & (of_e(next_of) < n_exp)
    clo = jnp.where(has_next, chunks * step_in_e // steps_of_e, 0)
    chi = jnp.where(has_next, chunks * (step_in_e + 1) // steps_of_e, 0)
    nxt = jnp.where(has_next, of_e(next_of), last_e)
    par = jnp.where(used, of_e(slot_of), jnp.sum(jnp.where(ids == last_e, slot_of, 0)))
    blk = jnp.where(used, blk, n_blocks - 1)
    e = jnp.where(used, e, last_e)
    lo = jnp.where(used, lo, 0)
    hi = jnp.where(used, hi, 0)
    return tuple(t.astype(I32) for t in (blk, e, lo, hi, par, nxt, clo, chi))


def _combine_body(slot_ref, next_slot_ref, w_ref, h2_ref, ys_ref, g_ref, o_ref, ybuf, sems, *, tt, d, tiles):
    half = d // 2
    i = pl.program_id(0)
    cur = i % 2

    def row_copies(slots, buf, r):
        return [pltpu.make_async_copy(ys_ref.at[pl.ds(slots[r * TOP_K + k], 1)],
                                      ybuf.at[buf, k, pl.ds(r, 1)], sems.at[buf])
                for k in range(TOP_K)]

    def start_tile(slots, buf):
        def start(r, carry):
            for k, cp in enumerate(row_copies(slots, buf, r)):
                cp.start(priority=k % DMA_PRIORITIES)
            return carry
        lax.fori_loop(0, tt, start, 0)

    @pl.when(i == 0)
    def _():
        start_tile(slot_ref, cur)

    @pl.when(i + 1 < tiles)
    def _():
        start_tile(next_slot_ref, 1 - cur)

    for k in range(TOP_K):
        pltpu.make_async_copy(ys_ref.at[pl.ds(0, tt)], ybuf.at[cur, k], sems.at[cur]).wait()

    acc_lo = h2_ref[:, :half]
    acc_hi = h2_ref[:, half:]
    for k in range(TOP_K):
        lo, hi = _unpack_halves(ybuf[cur, k])
        wk = w_ref[:, k:k + 1]
        acc_lo = acc_lo + wk * lo
        acc_hi = acc_hi + wk * hi
    ss = jnp.sum(acc_lo * acc_lo, axis=-1, keepdims=True) + jnp.sum(acc_hi * acc_hi, axis=-1, keepdims=True)
    inv = lax.rsqrt(ss / d + EPS)
    o_ref[:, :half] = acc_lo * inv * g_ref[:, :half]
    o_ref[:, half:] = acc_hi * inv * g_ref[:, half:]


def _moe_combine(slot_flat, w_tok, h2, ys, g):
    n, d = h2.shape
    half = d // 2
    tt = _tile(n, 128)
    tiles = n // tt
    vmem = (2 * TOP_K * tt * half * 4 + 4 * tt * d * 4) / MIB
    return pl.pallas_call(
        functools.partial(_combine_body, tt=tt, d=d, tiles=tiles),
        out_shape=jax.ShapeDtypeStruct((n, d), F32),
        grid=(tiles,),
        in_specs=[
            pl.BlockSpec((tt * TOP_K,), lambda i: (i,), memory_space=pltpu.SMEM),
            pl.BlockSpec((tt * TOP_K,), lambda i: (jnp.minimum(i + 1, tiles - 1),), memory_space=pltpu.SMEM),
            pl.BlockSpec((tt, TOP_K), lambda i: (i, 0)),
            pl.BlockSpec((tt, d), lambda i: (i, 0)),
            pl.BlockSpec(memory_space=pl.ANY),
            pl.BlockSpec((1, d), lambda i: (0, 0)),
        ],
        out_specs=pl.BlockSpec((tt, d), lambda i: (i, 0)),
        scratch_shapes=[pltpu.VMEM((2, TOP_K, tt, half), U32), pltpu.SemaphoreType.DMA((2,))],
        compiler_params=_compiler_params(("arbitrary",), vmem),
        name="moe_combine",
    )(slot_flat, slot_flat, w_tok, h2, ys, g)


def _layer(h, l, p, dims):
    b, s, d = h.shape
    n = b * s
    x2 = h.reshape(n, d)
    aw, kvw, cw = dims["attn"], dims["kv"], dims["conv"]
    _, k_off, v_off, cv_off, cg_off, ga_off, gc_off = dims["src_offsets"]

    proj = _in_projection(x2, p["attn_norm_g"][l].reshape(1, d), p["w_in"][l].astype(BF16))
    attn = _window_attention(proj, p["sink_logits"][l], seq=s, attn_width=aw, kv_width=kvw,
                             k_off=k_off, v_off=v_off)
    conv = _conformer_conv(proj, p["conv_dw_w"][l], p["conv_dw_b"][l], p["conv_ln_g"][l],
                           p["conv_ln_b"][l], seq=s, a_off=cv_off, b_off=cg_off)
    merged = _gated_merge(attn, conv, p["w_o_attn"][l].astype(BF16), p["w_o_conv"][l].astype(BF16),
                          proj, ga_off=ga_off, gc_off=gc_off)
    h1 = _out_projection(merged, p["w_out"][l].astype(BF16), x2)

    n_exp = p["w_router"].shape[-1]
    xp, eidx, rank, wts, cnt = _router(
        h1, p["ffn_norm_g"][l].reshape(1, d), p["w_router"][l].T, p["router_bias"][l].reshape(n_exp, 1))
    h2 = _shared_expert(xp, h1, p["w_sh_gate"][l].astype(BF16), p["w_sh_up"][l].astype(BF16),
                        p["w_sh_down"][l].astype(BF16))

    counts = cnt[:, 0].astype(I32)
    starts = jnp.cumsum(counts) - counts
    is_e = eidx[None, :, :] == jnp.arange(n_exp, dtype=I32)[:, None, None]
    slot = rank + jnp.sum(jnp.where(is_e, starts[:, None, None], 0), axis=0)
    slot_flat = slot.T.reshape(-1)
    rb = _tile(n * TOP_K, 256)
    visits = _visit_tables(counts, n * TOP_K, rb, _weight_chunks(p["w_exp_gate"].shape[-1]))

    xs = _moe_dispatch(slot_flat, xp)
    ys = _expert_ffn(visits, xs, p["w_exp_gate"][l], p["w_exp_up"][l], p["w_exp_down"][l], rb=rb)
    return slot_flat, wts.T, h2, ys


def kernel(x, attn_norm_g, w_in, sink_logits, w_o_attn, conv_dw_w, conv_dw_b, conv_ln_g, conv_ln_b,
           w_o_conv, w_out, ffn_norm_g, w_router, router_bias, w_exp_gate, w_exp_up, w_exp_down,
           w_sh_gate, w_sh_up, w_sh_down, final_norm_g):
    b, s, d = x.shape
    depth = w_in.shape[0]
    assert depth == 1, "the final RMSNorm is fused into the only layer's MoE combine"
    aw = w_o_attn.shape[1]
    cw = conv_dw_w.shape[2]
    kvw = (w_in.shape[2] - aw - 2 * cw - 2 * d) // 2
    assert s % Q_BLOCK == 0 and WINDOW <= Q_BLOCK
    src = [0, aw, aw + kvw, aw + 2 * kvw, aw + 2 * kvw + cw, aw + 2 * kvw + 2 * cw, aw + 2 * kvw + 2 * cw + d]
    dims = {"attn": aw, "kv": kvw, "conv": cw, "src_offsets": src}
    p = dict(attn_norm_g=attn_norm_g, w_in=w_in, sink_logits=sink_logits, w_o_attn=w_o_attn,
             conv_dw_w=conv_dw_w, conv_dw_b=conv_dw_b, conv_ln_g=conv_ln_g, conv_ln_b=conv_ln_b,
             w_o_conv=w_o_conv, w_out=w_out, ffn_norm_g=ffn_norm_g, w_router=w_router,
             router_bias=router_bias, w_exp_gate=w_exp_gate, w_exp_up=w_exp_up, w_exp_down=w_exp_down,
             w_sh_gate=w_sh_gate, w_sh_up=w_sh_up, w_sh_down=w_sh_down)
    slot_flat, w_tok, h2, ys = _layer(x, 0, p, dims)
    out = _moe_combine(slot_flat, w_tok, h2, ys, final_norm_g.reshape(1, d))
    return out.reshape(b, s, d)
```

```python
import functools

import jax
import jax.numpy as jnp
from jax import lax
from jax.experimental import pallas as pl
from jax.experimental.pallas import tpu as pltpu

F32, BF16, U32, I32 = jnp.float32, jnp.bfloat16, jnp.uint32, jnp.int32

EPS = 1e-6
WINDOW = 128
Q_BLOCK = 128
TOP_K = 8
N_GROUPS = 8
TOPK_GROUPS = 4
ROUTED_SCALE = 2.5

V7X_VMEM_BYTES = 64 * 1024 * 1024
V7X_BF16_SUBLANES = 16
MIB = 1024 * 1024
VMEM_CEILING_BYTES = V7X_VMEM_BYTES - 6 * MIB
VMEM_TEMPORARIES_MIB = 16
NORM_ROWS = 32
DMA_PRIORITIES = 2


def _compiler_params(semantics, window_mib):
    return pltpu.CompilerParams(
        dimension_semantics=semantics,
        vmem_limit_bytes=min(int((window_mib + VMEM_TEMPORARIES_MIB) * MIB), VMEM_CEILING_BYTES),
    )


def _tile(dim, target):
    t = min(dim, target)
    while dim % t:
        t //= 2
    return t


def _sigmoid(v):
    return 1.0 / (1.0 + jnp.exp(-v))


def _pack_halves(v_f32):
    bits = lax.bitcast_convert_type(v_f32, U32)
    h = v_f32.shape[1] // 2
    return (bits[:, :h] >> 16) | bits[:, h:]


def _unpack_halves(p_u32):
    lo = lax.bitcast_convert_type(p_u32 << 16, F32)
    hi = lax.bitcast_convert_type(p_u32 & jnp.uint32(0xFFFF0000), F32)
    return lo, hi


def _inproj_body(x_ref, g_ref, w_ref, o_ref, xn_ref):
    @pl.when(pl.program_id(1) == 0)
    def _():
        def norm_rows(c, carry):
            rows = pl.ds(pl.multiple_of(c * NORM_ROWS, NORM_ROWS), NORM_ROWS)
            x = x_ref[rows, :]
            ms = jnp.mean(x * x, axis=-1, keepdims=True)
            xn_ref[rows, :] = (x * lax.rsqrt(ms + EPS) * g_ref[...]).astype(BF16)
            return carry

        lax.fori_loop(0, x_ref.shape[0] // NORM_ROWS, norm_rows, 0)

    o_ref[...] = jnp.dot(xn_ref[...], w_ref[...], preferred_element_type=F32).astype(o_ref.dtype)


def _in_projection(x2, g, w):
    n, d = x2.shape
    wid = w.shape[1]
    tm, tn = _tile(n, 512), _tile(wid, 1024)
    vmem = (2 * tm * d * 4 + tm * d * 2 + 2 * d * tn * 2 + 2 * tm * tn * 2) / MIB
    return pl.pallas_call(
        _inproj_body,
        out_shape=jax.ShapeDtypeStruct((n, wid), BF16),
        grid=(n // tm, wid // tn),
        in_specs=[
            pl.BlockSpec((tm, d), lambda i, j: (i, 0)),
            pl.BlockSpec((1, d), lambda i, j: (0, 0)),
            pl.BlockSpec((d, tn), lambda i, j: (0, j)),
        ],
        out_specs=pl.BlockSpec((tm, tn), lambda i, j: (i, j)),
        scratch_shapes=[pltpu.VMEM((tm, d), BF16)],
        compiler_params=_compiler_params(("parallel", "arbitrary"), vmem),
        name="in_projection",
    )(x2, g, w)


ATTN_BLOCKS_PER_STEP = 2


def _attn_body(sink_ref, q_ref, kp_ref, kc_ref, kn_ref, vp_ref, vc_ref, vn_ref, o_ref, s_ref, p_ref, *,
               blocks_per_seq, n_heads, n_kv_heads, head_dim, qb):
    tq = Q_BLOCK
    k = jnp.concatenate([kp_ref[...], kc_ref[...], kn_ref[...]], axis=0)
    v = jnp.concatenate([vp_ref[...], vc_ref[...], vn_ref[...]], axis=0)
    qi = lax.broadcasted_iota(I32, (tq, 3 * tq), 0)
    ki = lax.broadcasted_iota(I32, (tq, 3 * tq), 1)
    rel = ki - tq - qi
    in_window = jnp.abs(rel) <= WINDOW
    dist = jnp.abs(rel).astype(F32)
    valid = []
    for b in range(qb):
        nb = (pl.program_id(0) * qb + b) % blocks_per_seq
        valid.append(in_window & ((ki >= tq) | (nb > 0)) & ((ki < 2 * tq) | (nb < blocks_per_seq - 1)))
    group = n_heads // n_kv_heads
    scale = head_dim ** -0.5
    pairs = [(h, b) for h in range(n_heads) for b in range(qb)]
    for idx, (h, b) in enumerate(pairs):
        j = h // group
        qh = q_ref[b * tq:(b + 1) * tq, h * head_dim:(h + 1) * head_dim]
        kj = k[b * tq:(b + 3) * tq, j * head_dim:(j + 1) * head_dim]
        s_ref[idx] = lax.dot_general(qh, kj, (((1,), (1,)), ((), ())), preferred_element_type=F32)
    for idx, (h, b) in enumerate(pairs):
        slope = 2.0 ** (-8.0 * (h + 1) / n_heads)
        sink = sink_ref[h]
        s = jnp.where(valid[b], s_ref[idx] * scale - slope * dist, -jnp.inf)
        m = jnp.maximum(jnp.max(s, axis=-1, keepdims=True), sink)
        e = jnp.exp(s - m)
        denom = jnp.sum(e, axis=-1, keepdims=True) + jnp.exp(sink - m)
        p_ref[idx] = (e * (1.0 / denom)).astype(BF16)
    for idx, (h, b) in enumerate(pairs):
        j = h // group
        vj = v[b * tq:(b + 3) * tq, j * head_dim:(j + 1) * head_dim]
        pv = jnp.dot(p_ref[idx], vj, preferred_element_type=F32)
        o_ref[b * tq:(b + 1) * tq, h * head_dim:(h + 1) * head_dim] = pv.astype(o_ref.dtype)


def _window_attention(proj, sink, *, seq, attn_width, kv_width, k_off, v_off):
    n = proj.shape[0]
    n_heads = sink.shape[0]
    head_dim = attn_width // n_heads
    n_kv_heads = kv_width // head_dim
    tq = Q_BLOCK
    nblk = n // tq
    qb = ATTN_BLOCKS_PER_STEP if nblk % ATTN_BLOCKS_PER_STEP == 0 else 1
    kb, vb = k_off // kv_width, v_off // kv_width
    body = functools.partial(_attn_body, blocks_per_seq=seq // tq, n_heads=n_heads,
                             n_kv_heads=n_kv_heads, head_dim=head_dim, qb=qb)

    def band(col):
        return [
            pl.BlockSpec((tq, kv_width), lambda i: (jnp.maximum(i * qb - 1, 0), col)),
            pl.BlockSpec((qb * tq, kv_width), lambda i: (i, col)),
            pl.BlockSpec((tq, kv_width), lambda i: (jnp.minimum((i + 1) * qb, nblk - 1), col)),
        ]

    return pl.pallas_call(
        body,
        out_shape=jax.ShapeDtypeStruct((n, attn_width), BF16),
        grid=(nblk // qb,),
        in_specs=[pl.BlockSpec(memory_space=pltpu.SMEM),
                  pl.BlockSpec((qb * tq, attn_width), lambda i: (i, 0))] + band(kb) + band(vb),
        out_specs=pl.BlockSpec((qb * tq, attn_width), lambda i: (i, 0)),
        scratch_shapes=[pltpu.VMEM((n_heads * qb, tq, 3 * tq), F32), pltpu.VMEM((n_heads * qb, tq, 3 * tq), BF16)],
        compiler_params=_compiler_params(("parallel",), 8 + n_heads * qb * tq * 3 * tq * 6 / MIB),
        name="window_attention",
    )(sink, proj, proj, proj, proj, proj, proj, proj)


CONV_ROWS = 64
CONV_LANES = 256


def _conv_body(*refs, tiles_per_seq, tt, c, cb, ksize):
    pieces = c // cb
    a_refs, b_refs = refs[:3 * pieces], refs[3 * pieces:6 * pieces]
    w_ref, bias_ref, lg_ref, lb_ref, o_ref, u_ref, shift_ref, y_ref = refs[6 * pieces:]
    ti = pl.program_id(0) % tiles_per_seq
    halo = V7X_BF16_SUBLANES

    def glu(a_ref, b_ref):
        return a_ref[...].astype(F32) * _sigmoid(b_ref[...].astype(F32))

    for p in range(pieces):
        ap_ref, ac_ref, an_ref = a_refs[3 * p:3 * p + 3]
        bp_ref, bc_ref, bn_ref = b_refs[3 * p:3 * p + 3]
        cols = slice(p * cb, (p + 1) * cb)
        u_ref[0:halo, cols] = jnp.where(ti > 0, glu(ap_ref, bp_ref), 0.0)
        u_ref[halo:halo + tt, cols] = glu(ac_ref, bc_ref)
        u_ref[halo + tt:, cols] = jnp.where(ti < tiles_per_seq - 1, glu(an_ref, bn_ref), 0.0)

    first = halo - ksize // 2
    n_rows = tt + 2 * halo
    cl, rr = min(c, CONV_LANES), min(tt, CONV_ROWS)
    for c0 in range(0, c, cl):
        lanes = slice(c0, c0 + cl)
        tile = u_ref[:, lanes]
        for phase in range(1, 8):
            shift_ref[phase - 1, :, lanes] = pltpu.roll(tile, n_rows - phase, 0)
        for r0 in range(0, tt, rr):
            acc = jnp.zeros((rr, cl), F32)
            for j in range(ksize):
                phase, a = (j + first) % 8, r0 + (j + first) // 8 * 8
                assert a + rr <= n_rows - 8
                src = u_ref[a:a + rr, lanes] if phase == 0 else shift_ref[phase - 1, a:a + rr, lanes]
                acc = acc + src * w_ref[j:j + 1, lanes]
            y_ref[r0:r0 + rr, lanes] = acc + bias_ref[:, lanes]

    y = y_ref[...]
    mu = jnp.mean(y, axis=-1, keepdims=True)
    yc = y - mu
    var = jnp.mean(yc * yc, axis=-1, keepdims=True)
    z = yc * lax.rsqrt(var + EPS) * lg_ref[...] + lb_ref[...]
    o_ref[...] = (z * _sigmoid(z)).astype(o_ref.dtype)


def _conformer_conv(proj, w_dw, b_dw, ln_g, ln_b, *, seq, a_off, b_off):
    n = proj.shape[0]
    ksize, c = w_dw.shape
    halo = V7X_BF16_SUBLANES
    assert ksize // 2 <= halo
    tt = _tile(seq, 128)
    cb = c
    while a_off % cb or b_off % cb:
        cb //= 2
    assert cb % 128 == 0 and c % cb == 0
    pieces = c // cb
    hb = tt // halo
    nhalo = n // halo
    body = functools.partial(_conv_body, tiles_per_seq=seq // tt, tt=tt, c=c, cb=cb, ksize=ksize)

    def band(col):
        return [
            pl.BlockSpec((halo, cb), lambda i: (jnp.maximum(i * hb - 1, 0), col)),
            pl.BlockSpec((tt, cb), lambda i: (i, col)),
            pl.BlockSpec((halo, cb), lambda i: (jnp.minimum((i + 1) * hb, nhalo - 1), col)),
        ]

    bands = []
    for off in (a_off, b_off):
        for p in range(pieces):
            bands += band(off // cb + p)
    row = lambda a: a.reshape(1, c)
    vec = pl.BlockSpec((1, c), lambda i: (0, 0))
    return pl.pallas_call(
        body,
        out_shape=jax.ShapeDtypeStruct((n, c), BF16),
        grid=(n // tt,),
        in_specs=bands + [pl.BlockSpec((ksize, c), lambda i: (0, 0)), vec, vec, vec],
        out_specs=pl.BlockSpec((tt, c), lambda i: (i, 0)),
        scratch_shapes=[pltpu.VMEM((tt + 2 * halo, c), F32), pltpu.VMEM((7, tt + 2 * halo, c), F32),
                        pltpu.VMEM((tt, c), F32)],
        compiler_params=_compiler_params(("parallel",), (8 * (tt + 2 * halo) + 5 * tt) * c * 4 / MIB),
        name="conformer_conv",
    )(*([proj] * (6 * pieces)), w_dw, row(b_dw), row(ln_g), row(ln_b))


def _merge_body(attn_ref, conv_ref, woa_ref, woc_ref, ga_ref, gc_ref, o_ref):
    ad = jnp.dot(attn_ref[...], woa_ref[...], preferred_element_type=F32)
    cd = jnp.dot(conv_ref[...], woc_ref[...], preferred_element_type=F32)
    merged = _sigmoid(ga_ref[...].astype(F32)) * ad + _sigmoid(gc_ref[...].astype(F32)) * cd
    o_ref[...] = merged.astype(o_ref.dtype)


def _gated_merge(attn, conv, w_oa, w_oc, proj, *, ga_off, gc_off):
    n, aw = attn.shape
    cw = conv.shape[1]
    d = w_oa.shape[1]
    tm, tn = _tile(n, 512), _tile(d, 1024)
    gab, gcb = ga_off // tn, gc_off // tn
    vmem = (2 * tm * (aw + cw) * 2 + 2 * (aw + cw) * tn * 2 + 6 * tm * tn * 2) / MIB
    return pl.pallas_call(
        _merge_body,
        out_shape=jax.ShapeDtypeStruct((n, d), BF16),
        grid=(n // tm, d // tn),
        in_specs=[
            pl.BlockSpec((tm, aw), lambda i, j: (i, 0)),
            pl.BlockSpec((tm, cw), lambda i, j: (i, 0)),
            pl.BlockSpec((aw, tn), lambda i, j: (0, j)),
            pl.BlockSpec((cw, tn), lambda i, j: (0, j)),
            pl.BlockSpec((tm, tn), lambda i, j: (i, gab + j)),
            pl.BlockSpec((tm, tn), lambda i, j: (i, gcb + j)),
        ],
        out_specs=pl.BlockSpec((tm, tn), lambda i, j: (i, j)),
        compiler_params=_compiler_params(("parallel", "parallel"), vmem),
        name="gated_merge",
    )(attn, conv, w_oa, w_oc, proj, proj)


def _outproj_body(m_ref, w_ref, x_ref, o_ref):
    o_ref[...] = x_ref[...] + jnp.dot(m_ref[...], w_ref[...], preferred_element_type=F32)


def _out_projection(merged, w_out, x2):
    n, d = merged.shape
    tm, tn = _tile(n, 512), _tile(d, 1024)
    vmem = (2 * tm * d * 2 + 2 * d * tn * 2 + 4 * tm * tn * 4) / MIB
    return pl.pallas_call(
        _outproj_body,
        out_shape=jax.ShapeDtypeStruct((n, d), F32),
        grid=(n // tm, d // tn),
        in_specs=[
            pl.BlockSpec((tm, d), lambda i, j: (i, 0)),
            pl.BlockSpec((d, tn), lambda i, j: (0, j)),
            pl.BlockSpec((tm, tn), lambda i, j: (i, j)),
        ],
        out_specs=pl.BlockSpec((tm, tn), lambda i, j: (i, j)),
        compiler_params=_compiler_params(("parallel", "parallel"), vmem),
        name="out_projection",
    )(merged, w_out, x2)


def _router_body(h_ref, g_ref, whi_ref, wlo_ref, rb_ref,
                 xp_ref, eidx_ref, rank_ref, wt_ref, cnt_ref, carry_ref, hhi_ref, hlo_ref, *, n_exp, tm):
    @pl.when(pl.program_id(0) == 0)
    def _():
        carry_ref[...] = jnp.zeros_like(carry_ref)

    def norm_rows(c, carry):
        rows = pl.ds(pl.multiple_of(c * NORM_ROWS, NORM_ROWS), NORM_ROWS)
        h = h_ref[rows, :]
        ms = jnp.mean(h * h, axis=-1, keepdims=True)
        hn = h * lax.rsqrt(ms + EPS) * g_ref[...]
        hi = hn.astype(BF16)
        hhi_ref[rows, :] = hi
        hlo_ref[rows, :] = (hn - hi.astype(F32)).astype(BF16)
        xp_ref[rows, :] = _pack_halves(hi.astype(F32))
        return carry

    lax.fori_loop(0, tm // NORM_ROWS, norm_rows, 0)

    nt = (((1,), (1,)), ((), ()))
    logits = (lax.dot_general(whi_ref[...], hhi_ref[...], nt, preferred_element_type=F32)
              + lax.dot_general(whi_ref[...], hlo_ref[...], nt, preferred_element_type=F32)
              + lax.dot_general(wlo_ref[...], hhi_ref[...], nt, preferred_element_type=F32)
              + lax.dot_general(wlo_ref[...], hlo_ref[...], nt, preferred_element_type=F32))
    score = _sigmoid(logits)
    choice = score + rb_ref[...]

    gsz = n_exp // N_GROUPS
    sub = lax.broadcasted_iota(I32, (gsz, tm), 0)
    groups, gscore = [], []
    for g in range(N_GROUPS):
        grp = choice[g * gsz:(g + 1) * gsz, :]
        m1 = jnp.max(grp, axis=0, keepdims=True)
        first = jnp.min(jnp.where(grp == m1, sub, gsz), axis=0, keepdims=True)
        m2 = jnp.max(jnp.where(sub == first, -jnp.inf, grp), axis=0, keepdims=True)
        groups.append(grp)
        gscore.append(m1 + m2)
    kept = []
    for g in range(N_GROUPS):
        beaten_by = jnp.zeros((1, tm), I32)
        for o in range(N_GROUPS):
            if o != g:
                wins = (gscore[o] >= gscore[g]) if o < g else (gscore[o] > gscore[g])
                beaten_by = beaten_by + wins.astype(I32)
        kept.append(jnp.where(beaten_by < TOPK_GROUPS, groups[g], -jnp.inf))
    cand = jnp.concatenate(kept, axis=0)

    eiota = lax.broadcasted_iota(I32, (n_exp, tm), 0)
    msel = jnp.zeros((n_exp, tm), F32)
    sel_idx, sel_score = [], []
    for _ in range(TOP_K):
        m = jnp.max(cand, axis=0, keepdims=True)
        first = jnp.min(jnp.where(cand == m, eiota, n_exp), axis=0, keepdims=True)
        hit = eiota == first
        sel_idx.append(first)
        sel_score.append(jnp.sum(jnp.where(hit, score, 0.0), axis=0, keepdims=True))
        msel = msel + hit.astype(F32)
        cand = jnp.where(hit, -jnp.inf, cand)

    rows = lax.broadcasted_iota(I32, (tm, tm), 0)
    cols = lax.broadcasted_iota(I32, (tm, tm), 1)
    earlier = (rows < cols).astype(BF16)
    before = jnp.dot(msel.astype(BF16), earlier, preferred_element_type=F32)
    rank_full = carry_ref[:, 0:1] + before
    total = carry_ref[...] + jnp.sum(msel, axis=1, keepdims=True)
    carry_ref[...] = total
    cnt_ref[...] = total

    wsum = sel_score[0]
    for k in range(1, TOP_K):
        wsum = wsum + sel_score[k]
    ranks = [jnp.sum(jnp.where(eiota == sel_idx[k], rank_full, 0.0), axis=0, keepdims=True)
             for k in range(TOP_K)]
    eidx_ref[...] = jnp.concatenate(sel_idx, axis=0)
    rank_ref[...] = jnp.concatenate(ranks, axis=0).astype(I32)
    wt_ref[...] = jnp.concatenate([sc / wsum * ROUTED_SCALE for sc in sel_score], axis=0)


def _const_spec(shape):
    return pl.BlockSpec(shape, lambda i: (0,) * len(shape), pipeline_mode=pl.Buffered(1))


def _router(h1, g, w_router_t, router_bias):
    n, d = h1.shape
    n_exp = w_router_t.shape[0]
    tm = _tile(n, 256)
    body = functools.partial(_router_body, n_exp=n_exp, tm=tm)
    w_hi = w_router_t.astype(BF16)
    w_lo = (w_router_t - w_hi.astype(F32)).astype(BF16)
    vmem = (2 * tm * d * 4 + 2 * tm * d * 2 + n_exp * d * 4 + tm * d * 4) / MIB
    return pl.pallas_call(
        body,
        out_shape=(
            jax.ShapeDtypeStruct((n, d // 2), U32),
            jax.ShapeDtypeStruct((TOP_K, n), I32),
            jax.ShapeDtypeStruct((TOP_K, n), I32),
            jax.ShapeDtypeStruct((TOP_K, n), F32),
            jax.ShapeDtypeStruct((n_exp, 128), F32),
        ),
        grid=(n // tm,),
        in_specs=[
            pl.BlockSpec((tm, d), lambda i: (i, 0)),
            _const_spec((1, d)),
            _const_spec((n_exp, d)),
            _const_spec((n_exp, d)),
            _const_spec((n_exp, 1)),
        ],
        out_specs=(
            pl.BlockSpec((tm, d // 2), lambda i: (i, 0)),
            pl.BlockSpec((TOP_K, tm), lambda i: (0, i)),
            pl.BlockSpec((TOP_K, tm), lambda i: (0, i)),
            pl.BlockSpec((TOP_K, tm), lambda i: (0, i)),
            pl.BlockSpec((n_exp, 128), lambda i: (0, 0)),
        ),
        scratch_shapes=[pltpu.VMEM((n_exp, 128), F32), pltpu.VMEM((tm, d), BF16), pltpu.VMEM((tm, d), BF16)],
        compiler_params=_compiler_params(("arbitrary",), vmem),
        name="router",
    )(h1, g, w_hi, w_lo, router_bias)


def _swiglu(x_packed, wg, wu, wd):
    lo, hi = _unpack_halves(x_packed)
    x = jnp.concatenate([lo.astype(BF16), hi.astype(BF16)], axis=1)
    gate = jnp.dot(x, wg, preferred_element_type=F32)
    up = jnp.dot(x, wu, preferred_element_type=F32)
    act = (gate * _sigmoid(gate) * up).astype(BF16)
    return jnp.dot(act, wd, preferred_element_type=F32)


def _shared_body(xp_ref, h1_ref, wg_ref, wu_ref, wd_ref, h2_ref):
    h2_ref[...] = h1_ref[...] + _swiglu(xp_ref[...], wg_ref[...], wu_ref[...], wd_ref[...])


def _shared_expert(xp, h1, wg, wu, wd):
    n, d = h1.shape
    f = wg.shape[1]
    tm = _tile(n, 256)
    vmem = (2 * tm * d * 2 + 4 * tm * d * 4 + 3 * d * f * 2) / MIB
    return pl.pallas_call(
        _shared_body,
        out_shape=jax.ShapeDtypeStruct((n, d), F32),
        grid=(n // tm,),
        in_specs=[
            pl.BlockSpec((tm, d // 2), lambda i: (i, 0)),
            pl.BlockSpec((tm, d), lambda i: (i, 0)),
            _const_spec((d, f)),
            _const_spec((d, f)),
            _const_spec((f, d)),
        ],
        out_specs=pl.BlockSpec((tm, d), lambda i: (i, 0)),
        compiler_params=_compiler_params(("parallel",), vmem),
        name="shared_expert",
    )(xp, h1, wg, wu, wd)


ROW_GROUP = 8


def _dispatch_body(slot_ref, xp_ref, xs_ref, sem, *, tt):
    def start(g, carry):
        base = pl.multiple_of(g * ROW_GROUP, ROW_GROUP)
        for s in range(ROW_GROUP):
            for k in range(TOP_K):
                pltpu.make_async_copy(xp_ref.at[pl.ds(base + s, 1)],
                                      xs_ref.at[pl.ds(slot_ref[(base + s) * TOP_K + k], 1)],
                                      sem).start(priority=k % DMA_PRIORITIES)
        return carry

    lax.fori_loop(0, tt // ROW_GROUP, start, 0)
    for _ in range(TOP_K):
        pltpu.make_async_copy(xp_ref, xs_ref.at[pl.ds(0, tt)], sem).wait()


def _moe_dispatch(slot_flat, xp):
    n, half = xp.shape
    tt = _tile(n, 256)
    return pl.pallas_call(
        functools.partial(_dispatch_body, tt=tt),
        out_shape=jax.ShapeDtypeStruct((n * TOP_K, half), U32),
        grid=(n // tt,),
        in_specs=[
            pl.BlockSpec((tt * TOP_K,), lambda i: (i,), memory_space=pltpu.SMEM),
            pl.BlockSpec((tt, half), lambda i: (i, 0)),
        ],
        out_specs=pl.BlockSpec(memory_space=pl.ANY),
        scratch_shapes=[pltpu.SemaphoreType.DMA(())],
        compiler_params=_compiler_params(("arbitrary",), 2 * tt * half * 4 / MIB),
        name="moe_dispatch",
    )(slot_flat, xp)


def _ffn_body(vblk_ref, vexp_ref, vlo_ref, vhi_ref, vpar_ref, nexp_ref, clo_ref, chi_ref,
              xs_ref, wg_hbm, wu_hbm, wd_hbm, ys_ref,
              wg_ref, wu_ref, wd_ref, sg_ref, su_ref, sd_ref, sems, *, rb, chunks):
    v = pl.program_id(0)
    mats = ((wg_hbm, wg_ref, sg_ref), (wu_hbm, wu_ref, su_ref), (wd_hbm, wd_ref, sd_ref))

    def chunk_copy(m, e, c, stage):
        hbm, _, stage_ref = mats[m]
        rows = stage_ref.shape[1]
        return pltpu.make_async_copy(hbm.at[e, pl.ds(c * rows, rows)], stage_ref.at[stage], sems.at[m, stage])

    def start_chunks(e, lo, hi):
        for m in range(3):
            for ahead in range(2):
                @pl.when(lo + ahead < hi)
                def _():
                    chunk_copy(m, e, lo + ahead, ahead).start()

    def convert_chunks(e, par, lo, hi):
        def convert(c, carry):
            stage = (c - lo) % 2
            for m in range(3):
                _, dst_ref, stage_ref = mats[m]
                rows = stage_ref.shape[1]
                chunk_copy(m, e, c, stage).wait()
                dst_ref[par, pl.ds(pl.multiple_of(c * rows, rows), rows), :] = stage_ref[stage].astype(BF16)

                @pl.when(c + 2 < hi)
                def _():
                    chunk_copy(m, e, c + 2, stage).start()
            return carry

        lax.fori_loop(lo, hi, convert, 0)

    @pl.when(v == 0)
    def _():
        start_chunks(vexp_ref[0], 0, chunks)
        convert_chunks(vexp_ref[0], vpar_ref[0], 0, chunks)

    start_chunks(nexp_ref[v], clo_ref[v], chi_ref[v])
    par = vpar_ref[v]
    y = _swiglu(xs_ref[...], wg_ref[par], wu_ref[par], wd_ref[par])
    packed = _pack_halves(y.astype(BF16).astype(F32))
    first_visit = (v == 0) | (vblk_ref[v] != vblk_ref[jnp.maximum(v - 1, 0)])

    @pl.when(first_visit)
    def _():
        ys_ref[...] = packed

    @pl.when(jnp.logical_not(first_visit))
    def _():
        rows = vblk_ref[v] * rb + lax.broadcasted_iota(I32, packed.shape, 0)
        mine = (rows >= vlo_ref[v]) & (rows < vhi_ref[v])
        ys_ref[...] = jnp.where(mine, packed, ys_ref[...])

    convert_chunks(nexp_ref[v], 1 - par, clo_ref[v], chi_ref[v])


def _weight_chunks(f):
    chunks = 16
    while f % (chunks * V7X_BF16_SUBLANES):
        chunks //= 2
    return chunks


def _expert_ffn(visits, xs, wg, wu, wd, *, rb):
    rows, half = xs.shape
    n_exp, d, f = wg.shape
    n_visits = visits[0].shape[0]
    chunks = _weight_chunks(f)
    vmem = (2 * 3 * d * f * 2 + 3 * 2 * (d // chunks) * f * 4 + 4 * rb * half * 4) / MIB
    block = lambda v, *tables: (tables[0][v], 0)
    grid_spec = pltpu.PrefetchScalarGridSpec(
        num_scalar_prefetch=len(visits),
        grid=(n_visits,),
        in_specs=[
            pl.BlockSpec((rb, half), block),
            pl.BlockSpec(memory_space=pl.ANY),
            pl.BlockSpec(memory_space=pl.ANY),
            pl.BlockSpec(memory_space=pl.ANY),
        ],
        out_specs=pl.BlockSpec((rb, half), block),
        scratch_shapes=[
            pltpu.VMEM((2, d, f), BF16), pltpu.VMEM((2, d, f), BF16), pltpu.VMEM((2, f, d), BF16),
            pltpu.VMEM((2, d // chunks, f), F32), pltpu.VMEM((2, d // chunks, f), F32),
            pltpu.VMEM((2, f // chunks, d), F32),
            pltpu.SemaphoreType.DMA((3, 2)),
        ],
    )
    return pl.pallas_call(
        functools.partial(_ffn_body, rb=rb, chunks=chunks),
        out_shape=jax.ShapeDtypeStruct((rows, half), U32),
        grid_spec=grid_spec,
        compiler_params=_compiler_params(("arbitrary",), vmem),
        name="expert_ffn",
    )(*visits, xs, wg, wu, wd)


def _visit_tables(counts, n_rows, rb, chunks):
    n_exp = counts.shape[0]
    n_blocks = n_rows // rb
    n_visits = n_blocks + n_exp
    ends = jnp.cumsum(counts)
    starts = ends - counts
    first_blk = starts // rb
    last_blk = jnp.maximum(ends - 1, starts) // rb
    per_exp = jnp.where(counts > 0, last_blk - first_blk + 1, 0)
    vis_end = jnp.cumsum(per_exp)
    vis_start = vis_end - per_exp
    total = vis_end[-1]
    v = jnp.arange(n_visits, dtype=I32)
    e = jnp.minimum(jnp.sum((v[:, None] >= vis_end[None, :]).astype(I32), axis=1), n_exp - 1)
    is_e = e[:, None] == jnp.arange(n_exp, dtype=I32)[None, :]
    of_e = lambda table: jnp.sum(jnp.where(is_e, table[None, :], 0), axis=1)
    blk = of_e(first_blk) + (v - of_e(vis_start))
    lo = jnp.maximum(of_e(starts), blk * rb)
    hi = jnp.minimum(of_e(ends), (blk + 1) * rb)
    used = v < total
    ids = jnp.arange(n_exp, dtype=I32)
    nonempty = counts > 0
    last_e = jnp.max(jnp.where(nonempty, ids, 0))
    slot_of = (jnp.cumsum(nonempty.astype(I32)) - 1) % 2
    later = (ids[None, :] > ids[:, None]) & nonempty[None, :]
    next_of = jnp.min(jnp.where(later, ids[None, :], n_exp), axis=1)
    step_in_e = v - of_e(vis_start)
    steps_of_e = jnp.maximum(of_e(per_exp), 1)
    has_next = used & (of_e(next_of) < n_exp)
    clo = jnp.where(has_next, chunks * step_in_e // steps_of_e, 0)
    chi = jnp.where(has_next, chunks * (step_in_e + 1) // steps_of_e, 0)
    nxt = jnp.where(has_next, of_e(next_of), last_e)
    par = jnp.where(used, of_e(slot_of), jnp.sum(jnp.where(ids == last_e, slot_of, 0)))
    blk = jnp.where(used, blk, n_blocks - 1)
    e = jnp.where(used, e, last_e)
    lo = jnp.where(used, lo, 0)
    hi = jnp.where(used, hi, 0)
    return tuple(t.astype(I32) for t in (blk, e, lo, hi, par, nxt, clo, chi))


def _combine_body(slot_ref, next_slot_ref, w_ref, h2_ref, ys_ref, g_ref, o_ref, ybuf, sems, *, tt, d, tiles):
    half = d // 2
    i = pl.program_id(0)
    cur = i % 2

    def start_tile(slots, buf):
        def start(g, carry):
            base = pl.multiple_of(g * ROW_GROUP, ROW_GROUP)
            for s in range(ROW_GROUP):
                for k in range(TOP_K):
                    pltpu.make_async_copy(ys_ref.at[pl.ds(slots[(base + s) * TOP_K + k], 1)],
                                          ybuf.at[buf, k, pl.ds(base + s, 1)],
                                          sems.at[buf]).start(priority=k % DMA_PRIORITIES)
            return carry
        lax.fori_loop(0, tt // ROW_GROUP, start, 0)

    @pl.when(i == 0)
    def _():
        start_tile(slot_ref, 0)

    for buf in range(2):
        @pl.when((i + 1 < tiles) & (cur == buf))
        def _():
            start_tile(next_slot_ref, 1 - buf)

    for k in range(TOP_K):
        pltpu.make_async_copy(ys_ref.at[pl.ds(0, tt)], ybuf.at[cur, k], sems.at[cur]).wait()

    acc_lo = h2_ref[:, :half]
    acc_hi = h2_ref[:, half:]
    for k in range(TOP_K):
        lo, hi = _unpack_halves(ybuf[cur, k])
        wk = w_ref[:, k:k + 1]
        acc_lo = acc_lo + wk * lo
        acc_hi = acc_hi + wk * hi
    ss = jnp.sum(acc_lo * acc_lo, axis=-1, keepdims=True) + jnp.sum(acc_hi * acc_hi, axis=-1, keepdims=True)
    inv = lax.rsqrt(ss / d + EPS)
    o_ref[:, :half] = acc_lo * inv * g_ref[:, :half]
    o_ref[:, half:] = acc_hi * inv * g_ref[:, half:]


def _moe_combine(slot_flat, w_tok, h2, ys, g):
    n, d = h2.shape
    half = d // 2
    tt = _tile(n, 128)
    tiles = n // tt
    vmem = (2 * TOP_K * tt * half * 4 + 4 * tt * d * 4) / MIB
    return pl.pallas_call(
        functools.partial(_combine_body, tt=tt, d=d, tiles=tiles),
        out_shape=jax.ShapeDtypeStruct((n, d), F32),
        grid=(tiles,),
        in_specs=[
            pl.BlockSpec((tt * TOP_K,), lambda i: (i,), memory_space=pltpu.SMEM),
            pl.BlockSpec((tt * TOP_K,), lambda i: (jnp.minimum(i + 1, tiles - 1),), memory_space=pltpu.SMEM),
            pl.BlockSpec((tt, TOP_K), lambda i: (i, 0)),
            pl.BlockSpec((tt, d), lambda i: (i, 0)),
            pl.BlockSpec(memory_space=pl.ANY),
            pl.BlockSpec((1, d), lambda i: (0, 0)),
        ],
        out_specs=pl.BlockSpec((tt, d), lambda i: (i, 0)),
        scratch_shapes=[pltpu.VMEM((2, TOP_K, tt, half), U32), pltpu.SemaphoreType.DMA((2,))],
        compiler_params=_compiler_params(("arbitrary",), vmem),
        name="moe_combine",
    )(slot_flat, slot_flat, w_tok, h2, ys, g)


def _layer(h, l, p, dims):
    b, s, d = h.shape
    n = b * s
    x2 = h.reshape(n, d)
    aw, kvw, cw = dims["attn"], dims["kv"], dims["conv"]
    _, k_off, v_off, cv_off, cg_off, ga_off, gc_off = dims["src_offsets"]

    proj = _in_projection(x2, p["attn_norm_g"][l].reshape(1, d), p["w_in"][l].astype(BF16))
    attn = _window_attention(proj, p["sink_logits"][l], seq=s, attn_width=aw, kv_width=kvw,
                             k_off=k_off, v_off=v_off)
    conv = _conformer_conv(proj, p["conv_dw_w"][l], p["conv_dw_b"][l], p["conv_ln_g"][l],
                           p["conv_ln_b"][l], seq=s, a_off=cv_off, b_off=cg_off)
    merged = _gated_merge(attn, conv, p["w_o_attn"][l].astype(BF16), p["w_o_conv"][l].astype(BF16),
                          proj, ga_off=ga_off, gc_off=gc_off)
    h1 = _out_projection(merged, p["w_out"][l].astype(BF16), x2)

    n_exp = p["w_router"].shape[-1]
    xp, eidx, rank, wts, cnt = _router(
        h1, p["ffn_norm_g"][l].reshape(1, d), p["w_router"][l].T, p["router_bias"][l].reshape(n_exp, 1))
    h2 = _shared_expert(xp, h1, p["w_sh_gate"][l].astype(BF16), p["w_sh_up"][l].astype(BF16),
                        p["w_sh_down"][l].astype(BF16))

    counts = cnt[:, 0].astype(I32)
    starts = jnp.cumsum(counts) - counts
    is_e = eidx[None, :, :] == jnp.arange(n_exp, dtype=I32)[:, None, None]
    slot = rank + jnp.sum(jnp.where(is_e, starts[:, None, None], 0), axis=0)
    slot_flat = slot.T.reshape(-1)
    rb = _tile(n * TOP_K, 256)
    visits = _visit_tables(counts, n * TOP_K, rb, _weight_chunks(p["w_exp_gate"].shape[-1]))

    xs = _moe_dispatch(slot_flat, xp)
    ys = _expert_ffn(visits, xs, p["w_exp_gate"][l], p["w_exp_up"][l], p["w_exp_down"][l], rb=rb)
    return slot_flat, wts.T, h2, ys


def kernel(x, attn_norm_g, w_in, sink_logits, w_o_attn, conv_dw_w, conv_dw_b, conv_ln_g, conv_ln_b,
           w_o_conv, w_out, ffn_norm_g, w_router, router_bias, w_exp_gate, w_exp_up, w_exp_down,
           w_sh_gate, w_sh_up, w_sh_down, final_norm_g):
    b, s, d = x.shape
    depth = w_in.shape[0]
    assert depth == 1, "the final RMSNorm is fused into the only layer's MoE combine"
    aw = w_o_attn.shape[1]
    cw = conv_dw_w.shape[2]
    kvw = (w_in.shape[2] - aw - 2 * cw - 2 * d) // 2
    assert s % Q_BLOCK == 0 and WINDOW <= Q_BLOCK
    src = [0, aw, aw + kvw, aw + 2 * kvw, aw + 2 * kvw + cw, aw + 2 * kvw + 2 * cw, aw + 2 * kvw + 2 * cw + d]
    dims = {"attn": aw, "kv": kvw, "conv": cw, "src_offsets": src}
    p = dict(attn_norm_g=attn_norm_g, w_in=w_in, sink_logits=sink_logits, w_o_attn=w_o_attn,
             conv_dw_w=conv_dw_w, conv_dw_b=conv_dw_b, conv_ln_g=conv_ln_g, conv_ln_b=conv_ln_b,
             w_o_conv=w_o_conv, w_out=w_out, ffn_norm_g=ffn_norm_g, w_router=w_router,
             router_bias=router_bias, w_exp_gate=w_exp_gate, w_exp_up=w_exp_up, w_exp_down=w_exp_down,
             w_sh_gate=w_sh_gate, w_sh_up=w_sh_up, w_sh_down=w_sh_down)
    slot_flat, w_tok, h2, ys = _layer(x, 0, p, dims)
    out = _moe_combine(slot_flat, w_tok, h2, ys, final_norm_g.reshape(1, d))
    return out.reshape(b, s, d)
```

```python
import functools

import jax
import jax.numpy as jnp
from jax import lax
from jax.experimental import pallas as pl
from jax.experimental.pallas import tpu as pltpu

F32, BF16, U32, I32 = jnp.float32, jnp.bfloat16, jnp.uint32, jnp.int32

EPS = 1e-6
WINDOW = 128
Q_BLOCK = 128
TOP_K = 8
N_GROUPS = 8
TOPK_GROUPS = 4
ROUTED_SCALE = 2.5

V7X_VMEM_BYTES = 64 * 1024 * 1024
V7X_BF16_SUBLANES = 16
MIB = 1024 * 1024
VMEM_CEILING_BYTES = V7X_VMEM_BYTES - 6 * MIB
VMEM_TEMPORARIES_MIB = 16
NORM_ROWS = 32
DMA_PRIORITIES = 2


def _compiler_params(semantics, window_mib):
    return pltpu.CompilerParams(
        dimension_semantics=semantics,
        vmem_limit_bytes=min(int((window_mib + VMEM_TEMPORARIES_MIB) * MIB), VMEM_CEILING_BYTES),
    )


def _tile(dim, target):
    t = min(dim, target)
    while dim % t:
        t //= 2
    return t


def _sigmoid(v):
    return 1.0 / (1.0 + jnp.exp(-v))


def _pack_halves(v_f32):
    bits = lax.bitcast_convert_type(v_f32, U32)
    h = v_f32.shape[1] // 2
    return (bits[:, :h] >> 16) | bits[:, h:]


def _unpack_halves(p_u32):
    lo = lax.bitcast_convert_type(p_u32 << 16, F32)
    hi = lax.bitcast_convert_type(p_u32 & jnp.uint32(0xFFFF0000), F32)
    return lo, hi


def _inproj_body(x_ref, g_ref, w_ref, o_ref, xn_ref):
    @pl.when(pl.program_id(1) == 0)
    def _():
        def norm_rows(c, carry):
            rows = pl.ds(pl.multiple_of(c * NORM_ROWS, NORM_ROWS), NORM_ROWS)
            x = x_ref[rows, :]
            ms = jnp.mean(x * x, axis=-1, keepdims=True)
            xn_ref[rows, :] = (x * lax.rsqrt(ms + EPS) * g_ref[...]).astype(BF16)
            return carry

        lax.fori_loop(0, x_ref.shape[0] // NORM_ROWS, norm_rows, 0, unroll=True)

    o_ref[...] = jnp.dot(xn_ref[...], w_ref[...], preferred_element_type=F32).astype(o_ref.dtype)


def _in_projection(x2, g, w):
    n, d = x2.shape
    wid = w.shape[1]
    tm, tn = _tile(n, 512), _tile(wid, 1024)
    vmem = (2 * tm * d * 4 + tm * d * 2 + 2 * d * tn * 2 + 2 * tm * tn * 2) / MIB
    return pl.pallas_call(
        _inproj_body,
        out_shape=jax.ShapeDtypeStruct((n, wid), BF16),
        grid=(n // tm, wid // tn),
        in_specs=[
            pl.BlockSpec((tm, d), lambda i, j: (i, 0)),
            pl.BlockSpec((1, d), lambda i, j: (0, 0)),
            pl.BlockSpec((d, tn), lambda i, j: (0, j)),
        ],
        out_specs=pl.BlockSpec((tm, tn), lambda i, j: (i, j)),
        scratch_shapes=[pltpu.VMEM((tm, d), BF16)],
        compiler_params=_compiler_params(("parallel", "arbitrary"), vmem),
        name="in_projection",
    )(x2, g, w)


ATTN_BLOCKS_PER_STEP = 2


def _attn_body(sink_ref, q_ref, kp_ref, kc_ref, kn_ref, vp_ref, vc_ref, vn_ref, o_ref, s_ref, p_ref, *,
               blocks_per_seq, n_heads, n_kv_heads, head_dim, qb):
    tq = Q_BLOCK
    k = jnp.concatenate([kp_ref[...], kc_ref[...], kn_ref[...]], axis=0)
    v = jnp.concatenate([vp_ref[...], vc_ref[...], vn_ref[...]], axis=0)
    qi = lax.broadcasted_iota(I32, (tq, 3 * tq), 0)
    ki = lax.broadcasted_iota(I32, (tq, 3 * tq), 1)
    rel = ki - tq - qi
    in_window = jnp.abs(rel) <= WINDOW
    dist = jnp.abs(rel).astype(F32)
    valid = []
    for b in range(qb):
        nb = (pl.program_id(0) * qb + b) % blocks_per_seq
        valid.append(in_window & ((ki >= tq) | (nb > 0)) & ((ki < 2 * tq) | (nb < blocks_per_seq - 1)))
    group = n_heads // n_kv_heads
    scale = head_dim ** -0.5
    pairs = [(h, b) for h in range(n_heads) for b in range(qb)]
    for idx, (h, b) in enumerate(pairs):
        j = h // group
        qh = q_ref[b * tq:(b + 1) * tq, h * head_dim:(h + 1) * head_dim]
        kj = k[b * tq:(b + 3) * tq, j * head_dim:(j + 1) * head_dim]
        s_ref[idx] = lax.dot_general(qh, kj, (((1,), (1,)), ((), ())), preferred_element_type=F32)
    for idx, (h, b) in enumerate(pairs):
        slope = 2.0 ** (-8.0 * (h + 1) / n_heads)
        sink = sink_ref[h]
        s = jnp.where(valid[b], s_ref[idx] * scale - slope * dist, -jnp.inf)
        m = jnp.maximum(jnp.max(s, axis=-1, keepdims=True), sink)
        e = jnp.exp(s - m)
        denom = jnp.sum(e, axis=-1, keepdims=True) + jnp.exp(sink - m)
        p_ref[idx] = (e * (1.0 / denom)).astype(BF16)
    for idx, (h, b) in enumerate(pairs):
        j = h // group
        vj = v[b * tq:(b + 3) * tq, j * head_dim:(j + 1) * head_dim]
        pv = jnp.dot(p_ref[idx], vj, preferred_element_type=F32)
        o_ref[b * tq:(b + 1) * tq, h * head_dim:(h + 1) * head_dim] = pv.astype(o_ref.dtype)


def _window_attention(proj, sink, *, seq, attn_width, kv_width, k_off, v_off):
    n = proj.shape[0]
    n_heads = sink.shape[0]
    head_dim = attn_width // n_heads
    n_kv_heads = kv_width // head_dim
    tq = Q_BLOCK
    nblk = n // tq
    qb = ATTN_BLOCKS_PER_STEP if nblk % ATTN_BLOCKS_PER_STEP == 0 else 1
    kb, vb = k_off // kv_width, v_off // kv_width
    body = functools.partial(_attn_body, blocks_per_seq=seq // tq, n_heads=n_heads,
                             n_kv_heads=n_kv_heads, head_dim=head_dim, qb=qb)

    def band(col):
        return [
            pl.BlockSpec((tq, kv_width), lambda i: (jnp.maximum(i * qb - 1, 0), col)),
            pl.BlockSpec((qb * tq, kv_width), lambda i: (i, col)),
            pl.BlockSpec((tq, kv_width), lambda i: (jnp.minimum((i + 1) * qb, nblk - 1), col)),
        ]

    return pl.pallas_call(
        body,
        out_shape=jax.ShapeDtypeStruct((n, attn_width), BF16),
        grid=(nblk // qb,),
        in_specs=[pl.BlockSpec(memory_space=pltpu.SMEM),
                  pl.BlockSpec((qb * tq, attn_width), lambda i: (i, 0))] + band(kb) + band(vb),
        out_specs=pl.BlockSpec((qb * tq, attn_width), lambda i: (i, 0)),
        scratch_shapes=[pltpu.VMEM((n_heads * qb, tq, 3 * tq), F32), pltpu.VMEM((n_heads * qb, tq, 3 * tq), BF16)],
        compiler_params=_compiler_params(("parallel",), 8 + n_heads * qb * tq * 3 * tq * 6 / MIB),
        name="window_attention",
    )(sink, proj, proj, proj, proj, proj, proj, proj)


CONV_ROWS = 64
CONV_LANES = 256


def _conv_body(*refs, tiles_per_seq, tt, c, cb, ksize):
    pieces = c // cb
    a_refs, b_refs = refs[:3 * pieces], refs[3 * pieces:6 * pieces]
    w_ref, bias_ref, lg_ref, lb_ref, o_ref, u_ref, shift_ref, y_ref = refs[6 * pieces:]
    ti = pl.program_id(0) % tiles_per_seq
    halo = V7X_BF16_SUBLANES

    def glu(a_ref, b_ref):
        return a_ref[...].astype(F32) * _sigmoid(b_ref[...].astype(F32))

    for p in range(pieces):
        ap_ref, ac_ref, an_ref = a_refs[3 * p:3 * p + 3]
        bp_ref, bc_ref, bn_ref = b_refs[3 * p:3 * p + 3]
        cols = slice(p * cb, (p + 1) * cb)
        u_ref[0:halo, cols] = jnp.where(ti > 0, glu(ap_ref, bp_ref), 0.0)
        u_ref[halo:halo + tt, cols] = glu(ac_ref, bc_ref)
        u_ref[halo + tt:, cols] = jnp.where(ti < tiles_per_seq - 1, glu(an_ref, bn_ref), 0.0)

    first = halo - ksize // 2
    n_rows = tt + 2 * halo
    cl, rr = min(c, CONV_LANES), min(tt, CONV_ROWS)
    for c0 in range(0, c, cl):
        lanes = slice(c0, c0 + cl)
        tile = u_ref[:, lanes]
        for phase in range(1, 8):
            shift_ref[phase - 1, :, lanes] = pltpu.roll(tile, n_rows - phase, 0)
        for r0 in range(0, tt, rr):
            acc = jnp.zeros((rr, cl), F32)
            for j in range(ksize):
                phase, a = (j + first) % 8, r0 + (j + first) // 8 * 8
                assert a + rr <= n_rows - 8
                src = u_ref[a:a + rr, lanes] if phase == 0 else shift_ref[phase - 1, a:a + rr, lanes]
                acc = acc + src * w_ref[j:j + 1, lanes]
            y_ref[r0:r0 + rr, lanes] = acc + bias_ref[:, lanes]

    y = y_ref[...]
    mu = jnp.mean(y, axis=-1, keepdims=True)
    yc = y - mu
    var = jnp.mean(yc * yc, axis=-1, keepdims=True)
    z = yc * lax.rsqrt(var + EPS) * lg_ref[...] + lb_ref[...]
    o_ref[...] = (z * _sigmoid(z)).astype(o_ref.dtype)


def _conformer_conv(proj, w_dw, b_dw, ln_g, ln_b, *, seq, a_off, b_off):
    n = proj.shape[0]
    ksize, c = w_dw.shape
    halo = V7X_BF16_SUBLANES
    assert ksize // 2 <= halo
    tt = _tile(seq, 128)
    cb = c
    while a_off % cb or b_off % cb:
        cb //= 2
    assert cb % 128 == 0 and c % cb == 0
    pieces = c // cb
    hb = tt // halo
    nhalo = n // halo
    body = functools.partial(_conv_body, tiles_per_seq=seq // tt, tt=tt, c=c, cb=cb, ksize=ksize)

    def band(col):
        return [
            pl.BlockSpec((halo, cb), lambda i: (jnp.maximum(i * hb - 1, 0), col)),
            pl.BlockSpec((tt, cb), lambda i: (i, col)),
            pl.BlockSpec((halo, cb), lambda i: (jnp.minimum((i + 1) * hb, nhalo - 1), col)),
        ]

    bands = []
    for off in (a_off, b_off):
        for p in range(pieces):
            bands += band(off // cb + p)
    row = lambda a: a.reshape(1, c)
    vec = pl.BlockSpec((1, c), lambda i: (0, 0))
    return pl.pallas_call(
        body,
        out_shape=jax.ShapeDtypeStruct((n, c), BF16),
        grid=(n // tt,),
        in_specs=bands + [pl.BlockSpec((ksize, c), lambda i: (0, 0)), vec, vec, vec],
        out_specs=pl.BlockSpec((tt, c), lambda i: (i, 0)),
        scratch_shapes=[pltpu.VMEM((tt + 2 * halo, c), F32), pltpu.VMEM((7, tt + 2 * halo, c), F32),
                        pltpu.VMEM((tt, c), F32)],
        compiler_params=_compiler_params(("parallel",), (8 * (tt + 2 * halo) + 5 * tt) * c * 4 / MIB),
        name="conformer_conv",
    )(*([proj] * (6 * pieces)), w_dw, row(b_dw), row(ln_g), row(ln_b))


def _merge_body(attn_ref, conv_ref, woa_ref, woc_ref, ga_ref, gc_ref, o_ref):
    ad = jnp.dot(attn_ref[...], woa_ref[...], preferred_element_type=F32)
    cd = jnp.dot(conv_ref[...], woc_ref[...], preferred_element_type=F32)
    merged = _sigmoid(ga_ref[...].astype(F32)) * ad + _sigmoid(gc_ref[...].astype(F32)) * cd
    o_ref[...] = merged.astype(o_ref.dtype)


def _gated_merge(attn, conv, w_oa, w_oc, proj, *, ga_off, gc_off):
    n, aw = attn.shape
    cw = conv.shape[1]
    d = w_oa.shape[1]
    tm, tn = _tile(n, 1024), _tile(d, 512)
    gab, gcb = ga_off // tn, gc_off // tn
    vmem = (2 * tm * (aw + cw) * 2 + 2 * (aw + cw) * tn * 2 + 6 * tm * tn * 2) / MIB
    return pl.pallas_call(
        _merge_body,
        out_shape=jax.ShapeDtypeStruct((n, d), BF16),
        grid=(n // tm, d // tn),
        in_specs=[
            pl.BlockSpec((tm, aw), lambda i, j: (i, 0)),
            pl.BlockSpec((tm, cw), lambda i, j: (i, 0)),
            pl.BlockSpec((aw, tn), lambda i, j: (0, j)),
            pl.BlockSpec((cw, tn), lambda i, j: (0, j)),
            pl.BlockSpec((tm, tn), lambda i, j: (i, gab + j)),
            pl.BlockSpec((tm, tn), lambda i, j: (i, gcb + j)),
        ],
        out_specs=pl.BlockSpec((tm, tn), lambda i, j: (i, j)),
        compiler_params=_compiler_params(("parallel", "parallel"), vmem),
        name="gated_merge",
    )(attn, conv, w_oa, w_oc, proj, proj)


def _outproj_body(m_ref, w_ref, x_ref, o_ref):
    o_ref[...] = x_ref[...] + jnp.dot(m_ref[...], w_ref[...], preferred_element_type=F32)


def _out_projection(merged, w_out, x2):
    n, d = merged.shape
    tm, tn = _tile(n, 1024), _tile(d, 512)
    vmem = (2 * tm * d * 2 + 2 * d * tn * 2 + 4 * tm * tn * 4) / MIB
    return pl.pallas_call(
        _outproj_body,
        out_shape=jax.ShapeDtypeStruct((n, d), F32),
        grid=(n // tm, d // tn),
        in_specs=[
            pl.BlockSpec((tm, d), lambda i, j: (i, 0)),
            pl.BlockSpec((d, tn), lambda i, j: (0, j)),
            pl.BlockSpec((tm, tn), lambda i, j: (i, j)),
        ],
        out_specs=pl.BlockSpec((tm, tn), lambda i, j: (i, j)),
        compiler_params=_compiler_params(("parallel", "parallel"), vmem),
        name="out_projection",
    )(merged, w_out, x2)


def _router_body(h_ref, g_ref, whi_ref, wlo_ref, rb_ref,
                 xp_ref, eidx_ref, rank_ref, wt_ref, cnt_ref, carry_ref, hhi_ref, hlo_ref, *, n_exp, tm):
    @pl.when(pl.program_id(0) == 0)
    def _():
        carry_ref[...] = jnp.zeros_like(carry_ref)

    def norm_rows(c, carry):
        rows = pl.ds(pl.multiple_of(c * NORM_ROWS, NORM_ROWS), NORM_ROWS)
        h = h_ref[rows, :]
        ms = jnp.mean(h * h, axis=-1, keepdims=True)
        hn = h * lax.rsqrt(ms + EPS) * g_ref[...]
        hi = hn.astype(BF16)
        hhi_ref[rows, :] = hi
        hlo_ref[rows, :] = (hn - hi.astype(F32)).astype(BF16)
        xp_ref[rows, :] = _pack_halves(hi.astype(F32))
        return carry

    lax.fori_loop(0, tm // NORM_ROWS, norm_rows, 0, unroll=True)

    nt = (((1,), (1,)), ((), ()))
    logits = (lax.dot_general(whi_ref[...], hhi_ref[...], nt, preferred_element_type=F32)
              + lax.dot_general(whi_ref[...], hlo_ref[...], nt, preferred_element_type=F32)
              + lax.dot_general(wlo_ref[...], hhi_ref[...], nt, preferred_element_type=F32)
              + lax.dot_general(wlo_ref[...], hlo_ref[...], nt, preferred_element_type=F32))
    score = _sigmoid(logits)
    choice = score + rb_ref[...]

    gsz = n_exp // N_GROUPS
    sub = lax.broadcasted_iota(I32, (gsz, tm), 0)
    groups, gscore = [], []
    for g in range(N_GROUPS):
        grp = choice[g * gsz:(g + 1) * gsz, :]
        m1 = jnp.max(grp, axis=0, keepdims=True)
        first = jnp.min(jnp.where(grp == m1, sub, gsz), axis=0, keepdims=True)
        m2 = jnp.max(jnp.where(sub == first, -jnp.inf, grp), axis=0, keepdims=True)
        groups.append(grp)
        gscore.append(m1 + m2)
    kept = []
    for g in range(N_GROUPS):
        beaten_by = jnp.zeros((1, tm), I32)
        for o in range(N_GROUPS):
            if o != g:
                wins = (gscore[o] >= gscore[g]) if o < g else (gscore[o] > gscore[g])
                beaten_by = beaten_by + wins.astype(I32)
        kept.append(jnp.where(beaten_by < TOPK_GROUPS, groups[g], -jnp.inf))
    cand = jnp.concatenate(kept, axis=0)

    eiota = lax.broadcasted_iota(I32, (n_exp, tm), 0)
    msel = jnp.zeros((n_exp, tm), F32)
    sel_idx, sel_score = [], []
    for _ in range(TOP_K):
        m = jnp.max(cand, axis=0, keepdims=True)
        first = jnp.min(jnp.where(cand == m, eiota, n_exp), axis=0, keepdims=True)
        hit = eiota == first
        sel_idx.append(first)
        sel_score.append(jnp.sum(jnp.where(hit, score, 0.0), axis=0, keepdims=True))
        msel = msel + hit.astype(F32)
        cand = jnp.where(hit, -jnp.inf, cand)

    rows = lax.broadcasted_iota(I32, (tm, tm), 0)
    cols = lax.broadcasted_iota(I32, (tm, tm), 1)
    earlier = (rows < cols).astype(BF16)
    before = jnp.dot(msel.astype(BF16), earlier, preferred_element_type=F32)
    rank_full = carry_ref[:, 0:1] + before
    total = carry_ref[...] + jnp.sum(msel, axis=1, keepdims=True)
    carry_ref[...] = total
    cnt_ref[...] = total

    wsum = sel_score[0]
    for k in range(1, TOP_K):
        wsum = wsum + sel_score[k]
    ranks = [jnp.sum(jnp.where(eiota == sel_idx[k], rank_full, 0.0), axis=0, keepdims=True)
             for k in range(TOP_K)]
    eidx_ref[...] = jnp.concatenate(sel_idx, axis=0)
    rank_ref[...] = jnp.concatenate(ranks, axis=0).astype(I32)
    wt_ref[...] = jnp.concatenate([sc / wsum * ROUTED_SCALE for sc in sel_score], axis=0)


def _const_spec(shape):
    return pl.BlockSpec(shape, lambda i: (0,) * len(shape), pipeline_mode=pl.Buffered(1))


def _router(h1, g, w_router_t, router_bias):
    n, d = h1.shape
    n_exp = w_router_t.shape[0]
    tm = _tile(n, 256)
    body = functools.partial(_router_body, n_exp=n_exp, tm=tm)
    w_hi = w_router_t.astype(BF16)
    w_lo = (w_router_t - w_hi.astype(F32)).astype(BF16)
    vmem = (2 * tm * d * 4 + 2 * tm * d * 2 + n_exp * d * 4 + tm * d * 4) / MIB
    return pl.pallas_call(
        body,
        out_shape=(
            jax.ShapeDtypeStruct((n, d // 2), U32),
            jax.ShapeDtypeStruct((TOP_K, n), I32),
            jax.ShapeDtypeStruct((TOP_K, n), I32),
            jax.ShapeDtypeStruct((TOP_K, n), F32),
            jax.ShapeDtypeStruct((n_exp, 128), F32),
        ),
        grid=(n // tm,),
        in_specs=[
            pl.BlockSpec((tm, d), lambda i: (i, 0)),
            _const_spec((1, d)),
            _const_spec((n_exp, d)),
            _const_spec((n_exp, d)),
            _const_spec((n_exp, 1)),
        ],
        out_specs=(
            pl.BlockSpec((tm, d // 2), lambda i: (i, 0)),
            pl.BlockSpec((TOP_K, tm), lambda i: (0, i)),
            pl.BlockSpec((TOP_K, tm), lambda i: (0, i)),
            pl.BlockSpec((TOP_K, tm), lambda i: (0, i)),
            pl.BlockSpec((n_exp, 128), lambda i: (0, 0)),
        ),
        scratch_shapes=[pltpu.VMEM((n_exp, 128), F32), pltpu.VMEM((tm, d), BF16), pltpu.VMEM((tm, d), BF16)],
        compiler_params=_compiler_params(("arbitrary",), vmem),
        name="router",
    )(h1, g, w_hi, w_lo, router_bias)


def _swiglu(x_packed, wg, wu, wd):
    lo, hi = _unpack_halves(x_packed)
    x = jnp.concatenate([lo.astype(BF16), hi.astype(BF16)], axis=1)
    gate = jnp.dot(x, wg, preferred_element_type=F32)
    up = jnp.dot(x, wu, preferred_element_type=F32)
    act = (gate * _sigmoid(gate) * up).astype(BF16)
    return jnp.dot(act, wd, preferred_element_type=F32)


def _shared_body(xp_ref, h1_ref, wg_ref, wu_ref, wd_ref, h2_ref):
    h2_ref[...] = h1_ref[...] + _swiglu(xp_ref[...], wg_ref[...], wu_ref[...], wd_ref[...])


def _shared_expert(xp, h1, wg, wu, wd):
    n, d = h1.shape
    f = wg.shape[1]
    tm = _tile(n, 256)
    vmem = (2 * tm * d * 2 + 4 * tm * d * 4 + 3 * d * f * 2) / MIB
    return pl.pallas_call(
        _shared_body,
        out_shape=jax.ShapeDtypeStruct((n, d), F32),
        grid=(n // tm,),
        in_specs=[
            pl.BlockSpec((tm, d // 2), lambda i: (i, 0)),
            pl.BlockSpec((tm, d), lambda i: (i, 0)),
            _const_spec((d, f)),
            _const_spec((d, f)),
            _const_spec((f, d)),
        ],
        out_specs=pl.BlockSpec((tm, d), lambda i: (i, 0)),
        compiler_params=_compiler_params(("parallel",), vmem),
        name="shared_expert",
    )(xp, h1, wg, wu, wd)


ROW_GROUP = 8


def _dispatch_body(slot_ref, xp_ref, xs_ref, sem, *, tt):
    def start(g, carry):
        base = pl.multiple_of(g * ROW_GROUP, ROW_GROUP)
        for s in range(ROW_GROUP):
            for k in range(TOP_K):
                pltpu.make_async_copy(xp_ref.at[pl.ds(base + s, 1)],
                                      xs_ref.at[pl.ds(slot_ref[(base + s) * TOP_K + k], 1)],
                                      sem).start(priority=k % DMA_PRIORITIES)
        return carry

    lax.fori_loop(0, tt // ROW_GROUP, start, 0)
    for _ in range(TOP_K):
        pltpu.make_async_copy(xp_ref, xs_ref.at[pl.ds(0, tt)], sem).wait()


def _moe_dispatch(slot_flat, xp):
    n, half = xp.shape
    tt = _tile(n, 256)
    return pl.pallas_call(
        functools.partial(_dispatch_body, tt=tt),
        out_shape=jax.ShapeDtypeStruct((n * TOP_K, half), U32),
        grid=(n // tt,),
        in_specs=[
            pl.BlockSpec((tt * TOP_K,), lambda i: (i,), memory_space=pltpu.SMEM),
            pl.BlockSpec((tt, half), lambda i: (i, 0)),
        ],
        out_specs=pl.BlockSpec(memory_space=pl.ANY),
        scratch_shapes=[pltpu.SemaphoreType.DMA(())],
        compiler_params=_compiler_params(("arbitrary",), 2 * tt * half * 4 / MIB),
        name="moe_dispatch",
    )(slot_flat, xp)


def _ffn_body(vblk_ref, vexp_ref, vlo_ref, vhi_ref, vpar_ref, nexp_ref, clo_ref, chi_ref,
              xs_ref, wg_hbm, wu_hbm, wd_hbm, ys_ref,
              wg_ref, wu_ref, wd_ref, sg_ref, su_ref, sd_ref, sems, *, rb, chunks):
    v = pl.program_id(0)
    mats = ((wg_hbm, wg_ref, sg_ref), (wu_hbm, wu_ref, su_ref), (wd_hbm, wd_ref, sd_ref))

    def chunk_copy(m, e, c, stage):
        hbm, _, stage_ref = mats[m]
        rows = stage_ref.shape[1]
        return pltpu.make_async_copy(hbm.at[e, pl.ds(c * rows, rows)], stage_ref.at[stage], sems.at[m, stage])

    def start_chunks(e, lo, hi):
        for m in range(3):
            for ahead in range(2):
                @pl.when(lo + ahead < hi)
                def _():
                    chunk_copy(m, e, lo + ahead, ahead).start()

    def convert_chunks(e, par, lo, hi):
        def convert(c, carry):
            stage = (c - lo) % 2
            for m in range(3):
                _, dst_ref, stage_ref = mats[m]
                rows = stage_ref.shape[1]
                chunk_copy(m, e, c, stage).wait()
                dst_ref[par, pl.ds(pl.multiple_of(c * rows, rows), rows), :] = stage_ref[stage].astype(BF16)

                @pl.when(c + 2 < hi)
                def _():
                    chunk_copy(m, e, c + 2, stage).start()
            return carry

        lax.fori_loop(lo, hi, convert, 0)

    @pl.when(v == 0)
    def _():
        start_chunks(vexp_ref[0], 0, chunks)
        convert_chunks(vexp_ref[0], vpar_ref[0], 0, chunks)

    start_chunks(nexp_ref[v], clo_ref[v], chi_ref[v])
    par = vpar_ref[v]
    y = _swiglu(xs_ref[...], wg_ref[par], wu_ref[par], wd_ref[par])
    packed = _pack_halves(y.astype(BF16).astype(F32))
    first_visit = (v == 0) | (vblk_ref[v] != vblk_ref[jnp.maximum(v - 1, 0)])

    @pl.when(first_visit)
    def _():
        ys_ref[...] = packed

    @pl.when(jnp.logical_not(first_visit))
    def _():
        rows = vblk_ref[v] * rb + lax.broadcasted_iota(I32, packed.shape, 0)
        mine = (rows >= vlo_ref[v]) & (rows < vhi_ref[v])
        ys_ref[...] = jnp.where(mine, packed, ys_ref[...])

    convert_chunks(nexp_ref[v], 1 - par, clo_ref[v], chi_ref[v])


def _weight_chunks(f):
    chunks = 16
    while f % (chunks * V7X_BF16_SUBLANES):
        chunks //= 2
    return chunks


def _expert_ffn(visits, xs, wg, wu, wd, *, rb):
    rows, half = xs.shape
    n_exp, d, f = wg.shape
    n_visits = visits[0].shape[0]
    chunks = _weight_chunks(f)
    vmem = (2 * 3 * d * f * 2 + 3 * 2 * (d // chunks) * f * 4 + 4 * rb * half * 4) / MIB
    block = lambda v, *tables: (tables[0][v], 0)
    grid_spec = pltpu.PrefetchScalarGridSpec(
        num_scalar_prefetch=len(visits),
        grid=(n_visits,),
        in_specs=[
            pl.BlockSpec((rb, half), block),
            pl.BlockSpec(memory_space=pl.ANY),
            pl.BlockSpec(memory_space=pl.ANY),
            pl.BlockSpec(memory_space=pl.ANY),
        ],
        out_specs=pl.BlockSpec((rb, half), block),
        scratch_shapes=[
            pltpu.VMEM((2, d, f), BF16), pltpu.VMEM((2, d, f), BF16), pltpu.VMEM((2, f, d), BF16),
            pltpu.VMEM((2, d // chunks, f), F32), pltpu.VMEM((2, d // chunks, f), F32),
            pltpu.VMEM((2, f // chunks, d), F32),
            pltpu.SemaphoreType.DMA((3, 2)),
        ],
    )
    return pl.pallas_call(
        functools.partial(_ffn_body, rb=rb, chunks=chunks),
        out_shape=jax.ShapeDtypeStruct((rows, half), U32),
        grid_spec=grid_spec,
        compiler_params=_compiler_params(("arbitrary",), vmem),
        name="expert_ffn",
    )(*visits, xs, wg, wu, wd)


def _visit_tables(counts, n_rows, rb, chunks):
    n_exp = counts.shape[0]
    n_blocks = n_rows // rb
    n_visits = n_blocks + n_exp
    ends = jnp.cumsum(counts)
    starts = ends - counts
    first_blk = starts // rb
    last_blk = jnp.maximum(ends - 1, starts) // rb
    per_exp = jnp.where(counts > 0, last_blk - first_blk + 1, 0)
    vis_end = jnp.cumsum(per_exp)
    vis_start = vis_end - per_exp
    total = vis_end[-1]
    v = jnp.arange(n_visits, dtype=I32)
    e = jnp.minimum(jnp.sum((v[:, None] >= vis_end[None, :]).astype(I32), axis=1), n_exp - 1)
    is_e = e[:, None] == jnp.arange(n_exp, dtype=I32)[None, :]
    of_e = lambda table: jnp.sum(jnp.where(is_e, table[None, :], 0), axis=1)
    blk = of_e(first_blk) + (v - of_e(vis_start))
    lo = jnp.maximum(of_e(starts), blk * rb)
    hi = jnp.minimum(of_e(ends), (blk + 1) * rb)
    used = v < total
    ids = jnp.arange(n_exp, dtype=I32)
    nonempty = counts > 0
    last_e = jnp.max(jnp.where(nonempty, ids, 0))
    slot_of = (jnp.cumsum(nonempty.astype(I32)) - 1) % 2
    later = (ids[None, :] > ids[:, None]) & nonempty[None, :]
    next_of = jnp.min(jnp.where(later, ids[None, :], n_exp), axis=1)
    step_in_e = v - of_e(vis_start)
    steps_of_e = jnp.maximum(of_e(per_exp), 1)
    has_next = used & (of_e(next_of) < n_exp)
    clo = jnp.where(has_next, chunks * step_in_e // steps_of_e, 0)
    chi = jnp.where(has_next, chunks * (step_in_e + 1) // steps_of_e, 0)
    nxt = jnp.where(has_next, of_e(next_of), last_e)
    par = jnp.where(used, of_e(slot_of), jnp.sum(jnp.where(ids == last_e, slot_of, 0)))
    blk = jnp.where(used, blk, n_blocks - 1)
    e = jnp.where(used, e, last_e)
    lo = jnp.where(used, lo, 0)
    hi = jnp.where(used, hi, 0)
    return tuple(t.astype(I32) for t in (blk, e, lo, hi, par, nxt, clo, chi))


def _combine_body(slot_ref, next_slot_ref, w_ref, h2_ref, ys_ref, g_ref, o_ref, ybuf, sems, *, tt, d, tiles):
    half = d // 2
    i = pl.program_id(0)
    cur = i % 2

    def start_tile(slots, buf):
        def start(g, carry):
            base = pl.multiple_of(g * ROW_GROUP, ROW_GROUP)
            for s in range(ROW_GROUP):
                for k in range(TOP_K):
                    pltpu.make_async_copy(ys_ref.at[pl.ds(slots[(base + s) * TOP_K + k], 1)],
                                          ybuf.at[buf, k, pl.ds(base + s, 1)],
                                          sems.at[buf]).start(priority=k % DMA_PRIORITIES)
            return carry
        lax.fori_loop(0, tt // ROW_GROUP, start, 0)

    @pl.when(i == 0)
    def _():
        start_tile(slot_ref, 0)

    for buf in range(2):
        @pl.when((i + 1 < tiles) & (cur == buf))
        def _():
            start_tile(next_slot_ref, 1 - buf)

    for k in range(TOP_K):
        pltpu.make_async_copy(ys_ref.at[pl.ds(0, tt)], ybuf.at[cur, k], sems.at[cur]).wait()

    acc_lo = h2_ref[:, :half]
    acc_hi = h2_ref[:, half:]
    for k in range(TOP_K):
        lo, hi = _unpack_halves(ybuf[cur, k])
        wk = w_ref[:, k:k + 1]
        acc_lo = acc_lo + wk * lo
        acc_hi = acc_hi + wk * hi
    ss = jnp.sum(acc_lo * acc_lo, axis=-1, keepdims=True) + jnp.sum(acc_hi * acc_hi, axis=-1, keepdims=True)
    inv = lax.rsqrt(ss / d + EPS)
    o_ref[:, :half] = acc_lo * inv * g_ref[:, :half]
    o_ref[:, half:] = acc_hi * inv * g_ref[:, half:]


def _moe_combine(slot_flat, w_tok, h2, ys, g):
    n, d = h2.shape
    half = d // 2
    tt = _tile(n, 128)
    tiles = n // tt
    vmem = (2 * TOP_K * tt * half * 4 + 4 * tt * d * 4) / MIB
    return pl.pallas_call(
        functools.partial(_combine_body, tt=tt, d=d, tiles=tiles),
        out_shape=jax.ShapeDtypeStruct((n, d), F32),
        grid=(tiles,),
        in_specs=[
            pl.BlockSpec((tt * TOP_K,), lambda i: (i,), memory_space=pltpu.SMEM),
            pl.BlockSpec((tt * TOP_K,), lambda i: (jnp.minimum(i + 1, tiles - 1),), memory_space=pltpu.SMEM),
            pl.BlockSpec((tt, TOP_K), lambda i: (i, 0)),
            pl.BlockSpec((tt, d), lambda i: (i, 0)),
            pl.BlockSpec(memory_space=pl.ANY),
            pl.BlockSpec((1, d), lambda i: (0, 0)),
        ],
        out_specs=pl.BlockSpec((tt, d), lambda i: (i, 0)),
        scratch_shapes=[pltpu.VMEM((2, TOP_K, tt, half), U32), pltpu.SemaphoreType.DMA((2,))],
        compiler_params=_compiler_params(("arbitrary",), vmem),
        name="moe_combine",
    )(slot_flat, slot_flat, w_tok, h2, ys, g)


def _layer(h, l, p, dims):
    b, s, d = h.shape
    n = b * s
    x2 = h.reshape(n, d)
    aw, kvw, cw = dims["attn"], dims["kv"], dims["conv"]
    _, k_off, v_off, cv_off, cg_off, ga_off, gc_off = dims["src_offsets"]

    proj = _in_projection(x2, p["attn_norm_g"][l].reshape(1, d), p["w_in"][l].astype(BF16))
    attn = _window_attention(proj, p["sink_logits"][l], seq=s, attn_width=aw, kv_width=kvw,
                             k_off=k_off, v_off=v_off)
    conv = _conformer_conv(proj, p["conv_dw_w"][l], p["conv_dw_b"][l], p["conv_ln_g"][l],
                           p["conv_ln_b"][l], seq=s, a_off=cv_off, b_off=cg_off)
    merged = _gated_merge(attn, conv, p["w_o_attn"][l].astype(BF16), p["w_o_conv"][l].astype(BF16),
                          proj, ga_off=ga_off, gc_off=gc_off)
    h1 = _out_projection(merged, p["w_out"][l].astype(BF16), x2)

    n_exp = p["w_router"].shape[-1]
    xp, eidx, rank, wts, cnt = _router(
        h1, p["ffn_norm_g"][l].reshape(1, d), p["w_router"][l].T, p["router_bias"][l].reshape(n_exp, 1))
    h2 = _shared_expert(xp, h1, p["w_sh_gate"][l].astype(BF16), p["w_sh_up"][l].astype(BF16),
                        p["w_sh_down"][l].astype(BF16))

    counts = cnt[:, 0].astype(I32)
    starts = jnp.cumsum(counts) - counts
    is_e = eidx[None, :, :] == jnp.arange(n_exp, dtype=I32)[:, None, None]
    slot = rank + jnp.sum(jnp.where(is_e, starts[:, None, None], 0), axis=0)
    slot_flat = slot.T.reshape(-1)
    rb = _tile(n * TOP_K, 256)
    visits = _visit_tables(counts, n * TOP_K, rb, _weight_chunks(p["w_exp_gate"].shape[-1]))

    xs = _moe_dispatch(slot_flat, xp)
    ys = _expert_ffn(visits, xs, p["w_exp_gate"][l], p["w_exp_up"][l], p["w_exp_down"][l], rb=rb)
    return slot_flat, wts.T, h2, ys


def kernel(x, attn_norm_g, w_in, sink_logits, w_o_attn, conv_dw_w, conv_dw_b, conv_ln_g, conv_ln_b,
           w_o_conv, w_out, ffn_norm_g, w_router, router_bias, w_exp_gate, w_exp_up, w_exp_down,
           w_sh_gate, w_sh_up, w_sh_down, final_norm_g):
    b, s, d = x.shape
    depth = w_in.shape[0]
    assert depth == 1, "the final RMSNorm is fused into the only layer's MoE combine"
    aw = w_o_attn.shape[1]
    cw = conv_dw_w.shape[2]
    kvw = (w_in.shape[2] - aw - 2 * cw - 2 * d) // 2
    assert s % Q_BLOCK == 0 and WINDOW <= Q_BLOCK
    src = [0, aw, aw + kvw, aw + 2 * kvw, aw + 2 * kvw + cw, aw + 2 * kvw + 2 * cw, aw + 2 * kvw + 2 * cw + d]
    dims = {"attn": aw, "kv": kvw, "conv": cw, "src_offsets": src}
    p = dict(attn_norm_g=attn_norm_g, w_in=w_in, sink_logits=sink_logits, w_o_attn=w_o_attn,
             conv_dw_w=conv_dw_w, conv_dw_b=conv_dw_b, conv_ln_g=conv_ln_g, conv_ln_b=conv_ln_b,
             w_o_conv=w_o_conv, w_out=w_out, ffn_norm_g=ffn_norm_g, w_router=w_router,
             router_bias=router_bias, w_exp_gate=w_exp_gate, w_exp_up=w_exp_up, w_exp_down=w_exp_down,
             w_sh_gate=w_sh_gate, w_sh_up=w_sh_up, w_sh_down=w_sh_down)
    slot_flat, w_tok, h2, ys = _layer(x, 0, p, dims)
    out = _moe_combine(slot_flat, w_tok, h2, ys, final_norm_g.reshape(1, d))
    return out.reshape(b, s, d)
```

```python
import functools

import jax
import jax.numpy as jnp
from jax import lax
from jax.experimental import pallas as pl
from jax.experimental.pallas import tpu as pltpu

F32, BF16, U32, I32 = jnp.float32, jnp.bfloat16, jnp.uint32, jnp.int32

EPS = 1e-6
WINDOW = 128
Q_BLOCK = 128
TOP_K = 8
N_GROUPS = 8
TOPK_GROUPS = 4
ROUTED_SCALE = 2.5

V7X_VMEM_BYTES = 64 * 1024 * 1024
V7X_BF16_SUBLANES = 16
MIB = 1024 * 1024
VMEM_CEILING_BYTES = V7X_VMEM_BYTES - 6 * MIB
VMEM_TEMPORARIES_MIB = 16
NORM_ROWS = 32
DMA_PRIORITIES = 2


def _compiler_params(semantics, window_mib):
    return pltpu.CompilerParams(
        dimension_semantics=semantics,
        vmem_limit_bytes=min(int((window_mib + VMEM_TEMPORARIES_MIB) * MIB), VMEM_CEILING_BYTES),
    )


def _tile(dim, target):
    t = min(dim, target)
    while dim % t:
        t //= 2
    return t


def _sigmoid(v):
    return 1.0 / (1.0 + jnp.exp(-v))


def _pack_halves(v_f32):
    bits = lax.bitcast_convert_type(v_f32, U32)
    h = v_f32.shape[1] // 2
    return (bits[:, :h] >> 16) | bits[:, h:]


def _unpack_halves(p_u32):
    lo = lax.bitcast_convert_type(p_u32 << 16, F32)
    hi = lax.bitcast_convert_type(p_u32 & jnp.uint32(0xFFFF0000), F32)
    return lo, hi


def _inproj_body(x_ref, g_ref, w_ref, o_ref, xn_ref):
    @pl.when(pl.program_id(1) == 0)
    def _():
        def norm_rows(c, carry):
            rows = pl.ds(pl.multiple_of(c * NORM_ROWS, NORM_ROWS), NORM_ROWS)
            x = x_ref[rows, :]
            ms = jnp.mean(x * x, axis=-1, keepdims=True)
            xn_ref[rows, :] = (x * lax.rsqrt(ms + EPS) * g_ref[...]).astype(BF16)
            return carry

        lax.fori_loop(0, x_ref.shape[0] // NORM_ROWS, norm_rows, 0, unroll=True)

    o_ref[...] = jnp.dot(xn_ref[...], w_ref[...], preferred_element_type=F32).astype(o_ref.dtype)


def _in_projection(x2, g, w):
    n, d = x2.shape
    wid = w.shape[1]
    tm, tn = _tile(n, 512), _tile(wid, 1024)
    vmem = (2 * tm * d * 4 + tm * d * 2 + 2 * d * tn * 2 + 2 * tm * tn * 2) / MIB
    return pl.pallas_call(
        _inproj_body,
        out_shape=jax.ShapeDtypeStruct((n, wid), BF16),
        grid=(n // tm, wid // tn),
        in_specs=[
            pl.BlockSpec((tm, d), lambda i, j: (i, 0)),
            pl.BlockSpec((1, d), lambda i, j: (0, 0)),
            pl.BlockSpec((d, tn), lambda i, j: (0, j)),
        ],
        out_specs=pl.BlockSpec((tm, tn), lambda i, j: (i, j)),
        scratch_shapes=[pltpu.VMEM((tm, d), BF16)],
        compiler_params=_compiler_params(("parallel", "arbitrary"), vmem),
        name="in_projection",
    )(x2, g, w)


ATTN_BLOCKS_PER_STEP = 2


def _attn_body(sink_ref, q_ref, kp_ref, kc_ref, kn_ref, vp_ref, vc_ref, vn_ref, o_ref, s_ref, p_ref, *,
               blocks_per_seq, n_heads, n_kv_heads, head_dim, qb):
    tq = Q_BLOCK
    k = jnp.concatenate([kp_ref[...], kc_ref[...], kn_ref[...]], axis=0)
    v = jnp.concatenate([vp_ref[...], vc_ref[...], vn_ref[...]], axis=0)
    qi = lax.broadcasted_iota(I32, (tq, 3 * tq), 0)
    ki = lax.broadcasted_iota(I32, (tq, 3 * tq), 1)
    rel = ki - tq - qi
    in_window = jnp.abs(rel) <= WINDOW
    dist = jnp.abs(rel).astype(F32)
    valid = []
    for b in range(qb):
        nb = (pl.program_id(0) * qb + b) % blocks_per_seq
        valid.append(in_window & ((ki >= tq) | (nb > 0)) & ((ki < 2 * tq) | (nb < blocks_per_seq - 1)))
    group = n_heads // n_kv_heads
    scale = head_dim ** -0.5
    pairs = [(h, b) for h in range(n_heads) for b in range(qb)]
    for idx, (h, b) in enumerate(pairs):
        j = h // group
        qh = q_ref[b * tq:(b + 1) * tq, h * head_dim:(h + 1) * head_dim]
        kj = k[b * tq:(b + 3) * tq, j * head_dim:(j + 1) * head_dim]
        s_ref[idx] = lax.dot_general(qh, kj, (((1,), (1,)), ((), ())), preferred_element_type=F32)
    for idx, (h, b) in enumerate(pairs):
        slope = 2.0 ** (-8.0 * (h + 1) / n_heads)
        sink = sink_ref[h]
        s = jnp.where(valid[b], s_ref[idx] * scale - slope * dist, -jnp.inf)
        m = jnp.maximum(jnp.max(s, axis=-1, keepdims=True), sink)
        e = jnp.exp(s - m)
        denom = jnp.sum(e, axis=-1, keepdims=True) + jnp.exp(sink - m)
        p_ref[idx] = (e * (1.0 / denom)).astype(BF16)
    for idx, (h, b) in enumerate(pairs):
        j = h // group
        vj = v[b * tq:(b + 3) * tq, j * head_dim:(j + 1) * head_dim]
        pv = jnp.dot(p_ref[idx], vj, preferred_element_type=F32)
        o_ref[b * tq:(b + 1) * tq, h * head_dim:(h + 1) * head_dim] = pv.astype(o_ref.dtype)


def _window_attention(proj, sink, *, seq, attn_width, kv_width, k_off, v_off):
    n = proj.shape[0]
    n_heads = sink.shape[0]
    head_dim = attn_width // n_heads
    n_kv_heads = kv_width // head_dim
    tq = Q_BLOCK
    nblk = n // tq
    qb = ATTN_BLOCKS_PER_STEP if nblk % ATTN_BLOCKS_PER_STEP == 0 else 1
    kb, vb = k_off // kv_width, v_off // kv_width
    body = functools.partial(_attn_body, blocks_per_seq=seq // tq, n_heads=n_heads,
                             n_kv_heads=n_kv_heads, head_dim=head_dim, qb=qb)

    def band(col):
        return [
            pl.BlockSpec((tq, kv_width), lambda i: (jnp.maximum(i * qb - 1, 0), col)),
            pl.BlockSpec((qb * tq, kv_width), lambda i: (i, col)),
            pl.BlockSpec((tq, kv_width), lambda i: (jnp.minimum((i + 1) * qb, nblk - 1), col)),
        ]

    return pl.pallas_call(
        body,
        out_shape=jax.ShapeDtypeStruct((n, attn_width), BF16),
        grid=(nblk // qb,),
        in_specs=[pl.BlockSpec(memory_space=pltpu.SMEM),
                  pl.BlockSpec((qb * tq, attn_width), lambda i: (i, 0))] + band(kb) + band(vb),
        out_specs=pl.BlockSpec((qb * tq, attn_width), lambda i: (i, 0)),
        scratch_shapes=[pltpu.VMEM((n_heads * qb, tq, 3 * tq), F32), pltpu.VMEM((n_heads * qb, tq, 3 * tq), BF16)],
        compiler_params=_compiler_params(("parallel",), 8 + n_heads * qb * tq * 3 * tq * 6 / MIB),
        name="window_attention",
    )(sink, proj, proj, proj, proj, proj, proj, proj)


CONV_ROWS = 64
CONV_LANES = 256


def _conv_body(*refs, tiles_per_seq, tt, c, cb, ksize):
    pieces = c // cb
    a_refs, b_refs = refs[:3 * pieces], refs[3 * pieces:6 * pieces]
    w_ref, bias_ref, lg_ref, lb_ref, o_ref, u_ref, shift_ref, y_ref = refs[6 * pieces:]
    ti = pl.program_id(0) % tiles_per_seq
    halo = V7X_BF16_SUBLANES

    def glu(a_ref, b_ref):
        return a_ref[...].astype(F32) * _sigmoid(b_ref[...].astype(F32))

    for p in range(pieces):
        ap_ref, ac_ref, an_ref = a_refs[3 * p:3 * p + 3]
        bp_ref, bc_ref, bn_ref = b_refs[3 * p:3 * p + 3]
        cols = slice(p * cb, (p + 1) * cb)
        u_ref[0:halo, cols] = jnp.where(ti > 0, glu(ap_ref, bp_ref), 0.0)
        u_ref[halo:halo + tt, cols] = glu(ac_ref, bc_ref)
        u_ref[halo + tt:, cols] = jnp.where(ti < tiles_per_seq - 1, glu(an_ref, bn_ref), 0.0)

    first = halo - ksize // 2
    n_rows = tt + 2 * halo
    cl, rr = min(c, CONV_LANES), min(tt, CONV_ROWS)
    for c0 in range(0, c, cl):
        lanes = slice(c0, c0 + cl)
        tile = u_ref[:, lanes]
        for phase in range(1, 8):
            shift_ref[phase - 1, :, lanes] = pltpu.roll(tile, n_rows - phase, 0)
        for r0 in range(0, tt, rr):
            acc = jnp.zeros((rr, cl), F32)
            for j in range(ksize):
                phase, a = (j + first) % 8, r0 + (j + first) // 8 * 8
                assert a + rr <= n_rows - 8
                src = u_ref[a:a + rr, lanes] if phase == 0 else shift_ref[phase - 1, a:a + rr, lanes]
                acc = acc + src * w_ref[j:j + 1, lanes]
            y_ref[r0:r0 + rr, lanes] = acc + bias_ref[:, lanes]

    y = y_ref[...]
    mu = jnp.mean(y, axis=-1, keepdims=True)
    yc = y - mu
    var = jnp.mean(yc * yc, axis=-1, keepdims=True)
    z = yc * lax.rsqrt(var + EPS) * lg_ref[...] + lb_ref[...]
    o_ref[...] = (z * _sigmoid(z)).astype(o_ref.dtype)


def _conformer_conv(proj, w_dw, b_dw, ln_g, ln_b, *, seq, a_off, b_off):
    n = proj.shape[0]
    ksize, c = w_dw.shape
    halo = V7X_BF16_SUBLANES
    assert ksize // 2 <= halo
    tt = _tile(seq, 128)
    cb = c
    while a_off % cb or b_off % cb:
        cb //= 2
    assert cb % 128 == 0 and c % cb == 0
    pieces = c // cb
    hb = tt // halo
    nhalo = n // halo
    body = functools.partial(_conv_body, tiles_per_seq=seq // tt, tt=tt, c=c, cb=cb, ksize=ksize)

    def band(col):
        return [
            pl.BlockSpec((halo, cb), lambda i: (jnp.maximum(i * hb - 1, 0), col)),
            pl.BlockSpec((tt, cb), lambda i: (i, col)),
            pl.BlockSpec((halo, cb), lambda i: (jnp.minimum((i + 1) * hb, nhalo - 1), col)),
        ]

    bands = []
    for off in (a_off, b_off):
        for p in range(pieces):
            bands += band(off // cb + p)
    row = lambda a: a.reshape(1, c)
    vec = pl.BlockSpec((1, c), lambda i: (0, 0))
    return pl.pallas_call(
        body,
        out_shape=jax.ShapeDtypeStruct((n, c), BF16),
        grid=(n // tt,),
        in_specs=bands + [pl.BlockSpec((ksize, c), lambda i: (0, 0)), vec, vec, vec],
        out_specs=pl.BlockSpec((tt, c), lambda i: (i, 0)),
        scratch_shapes=[pltpu.VMEM((tt + 2 * halo, c), F32), pltpu.VMEM((7, tt + 2 * halo, c), F32),
                        pltpu.VMEM((tt, c), F32)],
        compiler_params=_compiler_params(("parallel",), (8 * (tt + 2 * halo) + 5 * tt) * c * 4 / MIB),
        name="conformer_conv",
    )(*([proj] * (6 * pieces)), w_dw, row(b_dw), row(ln_g), row(ln_b))


def _merge_body(attn_ref, conv_ref, woa_ref, woc_ref, ga_ref, gc_ref, o_ref):
    ad = jnp.dot(attn_ref[...], woa_ref[...], preferred_element_type=F32)
    cd = jnp.dot(conv_ref[...], woc_ref[...], preferred_element_type=F32)
    merged = _sigmoid(ga_ref[...].astype(F32)) * ad + _sigmoid(gc_ref[...].astype(F32)) * cd
    o_ref[...] = merged.astype(o_ref.dtype)


def _gated_merge(attn, conv, w_oa, w_oc, proj, *, ga_off, gc_off):
    n, aw = attn.shape
    cw = conv.shape[1]
    d = w_oa.shape[1]
    tm, tn = _tile(n, 1024), _tile(d, 512)
    gab, gcb = ga_off // tn, gc_off // tn
    vmem = (2 * tm * (aw + cw) * 2 + 2 * (aw + cw) * tn * 2 + 6 * tm * tn * 2) / MIB
    return pl.pallas_call(
        _merge_body,
        out_shape=jax.ShapeDtypeStruct((n, d), BF16),
        grid=(n // tm, d // tn),
        in_specs=[
            pl.BlockSpec((tm, aw), lambda i, j: (i, 0)),
            pl.BlockSpec((tm, cw), lambda i, j: (i, 0)),
            pl.BlockSpec((aw, tn), lambda i, j: (0, j)),
            pl.BlockSpec((cw, tn), lambda i, j: (0, j)),
            pl.BlockSpec((tm, tn), lambda i, j: (i, gab + j)),
            pl.BlockSpec((tm, tn), lambda i, j: (i, gcb + j)),
        ],
        out_specs=pl.BlockSpec((tm, tn), lambda i, j: (i, j)),
        compiler_params=_compiler_params(("parallel", "parallel"), vmem),
        name="gated_merge",
    )(attn, conv, w_oa, w_oc, proj, proj)


def _outproj_body(m_ref, w_ref, x_ref, o_ref):
    o_ref[...] = x_ref[...] + jnp.dot(m_ref[...], w_ref[...], preferred_element_type=F32)


def _out_projection(merged, w_out, x2):
    n, d = merged.shape
    tm, tn = _tile(n, 1024), _tile(d, 512)
    vmem = (2 * tm * d * 2 + 2 * d * tn * 2 + 4 * tm * tn * 4) / MIB
    return pl.pallas_call(
        _outproj_body,
        out_shape=jax.ShapeDtypeStruct((n, d), F32),
        grid=(n // tm, d // tn),
        in_specs=[
            pl.BlockSpec((tm, d), lambda i, j: (i, 0)),
            pl.BlockSpec((d, tn), lambda i, j: (0, j)),
            pl.BlockSpec((tm, tn), lambda i, j: (i, j)),
        ],
        out_specs=pl.BlockSpec((tm, tn), lambda i, j: (i, j)),
        compiler_params=_compiler_params(("parallel", "parallel"), vmem),
        name="out_projection",
    )(merged, w_out, x2)


def _router_body(h_ref, g_ref, whi_ref, wlo_ref, rb_ref,
                 xp_ref, eidx_ref, rank_ref, wt_ref, cnt_ref, carry_ref, hhi_ref, hlo_ref, *, n_exp, tm):
    @pl.when(pl.program_id(0) == 0)
    def _():
        carry_ref[...] = jnp.zeros_like(carry_ref)

    def norm_rows(c, carry):
        rows = pl.ds(pl.multiple_of(c * NORM_ROWS, NORM_ROWS), NORM_ROWS)
        h = h_ref[rows, :]
        ms = jnp.mean(h * h, axis=-1, keepdims=True)
        hn = h * lax.rsqrt(ms + EPS) * g_ref[...]
        hi = hn.astype(BF16)
        hhi_ref[rows, :] = hi
        hlo_ref[rows, :] = (hn - hi.astype(F32)).astype(BF16)
        xp_ref[rows, :] = _pack_halves(hi.astype(F32))
        return carry

    lax.fori_loop(0, tm // NORM_ROWS, norm_rows, 0, unroll=True)

    nt = (((1,), (1,)), ((), ()))
    logits = (lax.dot_general(whi_ref[...], hhi_ref[...], nt, preferred_element_type=F32)
              + lax.dot_general(whi_ref[...], hlo_ref[...], nt, preferred_element_type=F32)
              + lax.dot_general(wlo_ref[...], hhi_ref[...], nt, preferred_element_type=F32)
              + lax.dot_general(wlo_ref[...], hlo_ref[...], nt, preferred_element_type=F32))
    score = _sigmoid(logits)
    choice = score + rb_ref[...]

    gsz = n_exp // N_GROUPS
    sub = lax.broadcasted_iota(I32, (gsz, tm), 0)
    groups, gscore = [], []
    for g in range(N_GROUPS):
        grp = choice[g * gsz:(g + 1) * gsz, :]
        m1 = jnp.max(grp, axis=0, keepdims=True)
        first = jnp.min(jnp.where(grp == m1, sub, gsz), axis=0, keepdims=True)
        m2 = jnp.max(jnp.where(sub == first, -jnp.inf, grp), axis=0, keepdims=True)
        groups.append(grp)
        gscore.append(m1 + m2)
    kept = []
    for g in range(N_GROUPS):
        beaten_by = jnp.zeros((1, tm), I32)
        for o in range(N_GROUPS):
            if o != g:
                wins = (gscore[o] >= gscore[g]) if o < g else (gscore[o] > gscore[g])
                beaten_by = beaten_by + wins.astype(I32)
        kept.append(jnp.where(beaten_by < TOPK_GROUPS, groups[g], -jnp.inf))
    cand = jnp.concatenate(kept, axis=0)

    eiota = lax.broadcasted_iota(I32, (n_exp, tm), 0)
    msel = jnp.zeros((n_exp, tm), F32)
    sel_idx, sel_score = [], []
    for _ in range(TOP_K):
        m = jnp.max(cand, axis=0, keepdims=True)
        first = jnp.min(jnp.where(cand == m, eiota, n_exp), axis=0, keepdims=True)
        hit = eiota == first
        sel_idx.append(first)
        sel_score.append(jnp.sum(jnp.where(hit, score, 0.0), axis=0, keepdims=True))
        msel = msel + hit.astype(F32)
        cand = jnp.where(hit, -jnp.inf, cand)

    rows = lax.broadcasted_iota(I32, (tm, tm), 0)
    cols = lax.broadcasted_iota(I32, (tm, tm), 1)
    earlier = (rows < cols).astype(BF16)
    before = jnp.dot(msel.astype(BF16), earlier, preferred_element_type=F32)
    rank_full = carry_ref[:, 0:1] + before
    total = carry_ref[...] + jnp.sum(msel, axis=1, keepdims=True)
    carry_ref[...] = total
    cnt_ref[...] = total

    wsum = sel_score[0]
    for k in range(1, TOP_K):
        wsum = wsum + sel_score[k]
    ranks = [jnp.sum(jnp.where(eiota == sel_idx[k], rank_full, 0.0), axis=0, keepdims=True)
             for k in range(TOP_K)]
    eidx_ref[...] = jnp.concatenate(sel_idx, axis=0)
    rank_ref[...] = jnp.concatenate(ranks, axis=0).astype(I32)
    wt_ref[...] = jnp.concatenate([sc / wsum * ROUTED_SCALE for sc in sel_score], axis=0)


def _const_spec(shape):
    return pl.BlockSpec(shape, lambda i: (0,) * len(shape), pipeline_mode=pl.Buffered(1))


def _router(h1, g, w_router_t, router_bias):
    n, d = h1.shape
    n_exp = w_router_t.shape[0]
    tm = _tile(n, 256)
    body = functools.partial(_router_body, n_exp=n_exp, tm=tm)
    w_hi = w_router_t.astype(BF16)
    w_lo = (w_router_t - w_hi.astype(F32)).astype(BF16)
    vmem = (2 * tm * d * 4 + 2 * tm * d * 2 + n_exp * d * 4 + tm * d * 4) / MIB
    return pl.pallas_call(
        body,
        out_shape=(
            jax.ShapeDtypeStruct((n, d // 2), U32),
            jax.ShapeDtypeStruct((TOP_K, n), I32),
            jax.ShapeDtypeStruct((TOP_K, n), I32),
            jax.ShapeDtypeStruct((TOP_K, n), F32),
            jax.ShapeDtypeStruct((n_exp, 128), F32),
        ),
        grid=(n // tm,),
        in_specs=[
            pl.BlockSpec((tm, d), lambda i: (i, 0)),
            _const_spec((1, d)),
            _const_spec((n_exp, d)),
            _const_spec((n_exp, d)),
            _const_spec((n_exp, 1)),
        ],
        out_specs=(
            pl.BlockSpec((tm, d // 2), lambda i: (i, 0)),
            pl.BlockSpec((TOP_K, tm), lambda i: (0, i)),
            pl.BlockSpec((TOP_K, tm), lambda i: (0, i)),
            pl.BlockSpec((TOP_K, tm), lambda i: (0, i)),
            pl.BlockSpec((n_exp, 128), lambda i: (0, 0)),
        ),
        scratch_shapes=[pltpu.VMEM((n_exp, 128), F32), pltpu.VMEM((tm, d), BF16), pltpu.VMEM((tm, d), BF16)],
        compiler_params=_compiler_params(("arbitrary",), vmem),
        name="router",
    )(h1, g, w_hi, w_lo, router_bias)


def _swiglu(x_packed, wg, wu, wd):
    lo, hi = _unpack_halves(x_packed)
    x = jnp.concatenate([lo.astype(BF16), hi.astype(BF16)], axis=1)
    gate = jnp.dot(x, wg, preferred_element_type=F32)
    up = jnp.dot(x, wu, preferred_element_type=F32)
    act = (gate * _sigmoid(gate) * up).astype(BF16)
    return jnp.dot(act, wd, preferred_element_type=F32)


def _shared_body(xp_ref, h1_ref, wg_ref, wu_ref, wd_ref, h2_ref):
    h2_ref[...] = h1_ref[...] + _swiglu(xp_ref[...], wg_ref[...], wu_ref[...], wd_ref[...])


def _shared_expert(xp, h1, wg, wu, wd):
    n, d = h1.shape
    f = wg.shape[1]
    tm = _tile(n, 256)
    vmem = (2 * tm * d * 2 + 4 * tm * d * 4 + 3 * d * f * 2) / MIB
    return pl.pallas_call(
        _shared_body,
        out_shape=jax.ShapeDtypeStruct((n, d), F32),
        grid=(n // tm,),
        in_specs=[
            pl.BlockSpec((tm, d // 2), lambda i: (i, 0)),
            pl.BlockSpec((tm, d), lambda i: (i, 0)),
            _const_spec((d, f)),
            _const_spec((d, f)),
            _const_spec((f, d)),
        ],
        out_specs=pl.BlockSpec((tm, d), lambda i: (i, 0)),
        compiler_params=_compiler_params(("parallel",), vmem),
        name="shared_expert",
    )(xp, h1, wg, wu, wd)


ROW_GROUP = 8


def _dispatch_body(slot_ref, xp_ref, xs_ref, sem, *, tt):
    def start(g, carry):
        base = pl.multiple_of(g * ROW_GROUP, ROW_GROUP)
        for s in range(ROW_GROUP):
            for k in range(TOP_K):
                pltpu.make_async_copy(xp_ref.at[pl.ds(base + s, 1)],
                                      xs_ref.at[pl.ds(slot_ref[(base + s) * TOP_K + k], 1)],
                                      sem).start(priority=k % DMA_PRIORITIES)
        return carry

    lax.fori_loop(0, tt // ROW_GROUP, start, 0)
    for _ in range(TOP_K):
        pltpu.make_async_copy(xp_ref, xs_ref.at[pl.ds(0, tt)], sem).wait()


def _moe_dispatch(slot_flat, xp):
    n, half = xp.shape
    tt = _tile(n, 256)
    return pl.pallas_call(
        functools.partial(_dispatch_body, tt=tt),
        out_shape=jax.ShapeDtypeStruct((n * TOP_K, half), U32),
        grid=(n // tt,),
        in_specs=[
            pl.BlockSpec((tt * TOP_K,), lambda i: (i,), memory_space=pltpu.SMEM),
            pl.BlockSpec((tt, half), lambda i: (i, 0)),
        ],
        out_specs=pl.BlockSpec(memory_space=pl.ANY),
        scratch_shapes=[pltpu.SemaphoreType.DMA(())],
        compiler_params=_compiler_params(("arbitrary",), 2 * tt * half * 4 / MIB),
        name="moe_dispatch",
    )(slot_flat, xp)


def _ffn_body(vblk_ref, vexp_ref, vlo_ref, vhi_ref, vpar_ref, nexp_ref, clo_ref, chi_ref,
              xs_ref, wg_hbm, wu_hbm, wd_hbm, ys_ref,
              wg_ref, wu_ref, wd_ref, sg_ref, su_ref, sd_ref, sems, *, rb, chunks):
    v = pl.program_id(0)
    mats = ((wg_hbm, wg_ref, sg_ref), (wu_hbm, wu_ref, su_ref), (wd_hbm, wd_ref, sd_ref))

    def chunk_copy(m, e, c, stage):
        hbm, _, stage_ref = mats[m]
        rows = stage_ref.shape[1]
        return pltpu.make_async_copy(hbm.at[e, pl.ds(c * rows, rows)], stage_ref.at[stage], sems.at[m, stage])

    def start_chunks(e, lo, hi):
        for m in range(3):
            for ahead in range(2):
                @pl.when(lo + ahead < hi)
                def _():
                    chunk_copy(m, e, lo + ahead, ahead).start()

    def convert_chunks(e, par, lo, hi):
        def convert(c, carry):
            stage = (c - lo) % 2
            for m in range(3):
                _, dst_ref, stage_ref = mats[m]
                rows = stage_ref.shape[1]
                chunk_copy(m, e, c, stage).wait()
                dst_ref[par, pl.ds(pl.multiple_of(c * rows, rows), rows), :] = stage_ref[stage].astype(BF16)

                @pl.when(c + 2 < hi)
                def _():
                    chunk_copy(m, e, c + 2, stage).start()
            return carry

        lax.fori_loop(lo, hi, convert, 0)

    @pl.when(v == 0)
    def _():
        start_chunks(vexp_ref[0], 0, chunks)
        convert_chunks(vexp_ref[0], vpar_ref[0], 0, chunks)

    start_chunks(nexp_ref[v], clo_ref[v], chi_ref[v])
    par = vpar_ref[v]
    y = _swiglu(xs_ref[...], wg_ref[par], wu_ref[par], wd_ref[par])
    packed = _pack_halves(y.astype(BF16).astype(F32))
    first_visit = (v == 0) | (vblk_ref[v] != vblk_ref[jnp.maximum(v - 1, 0)])

    @pl.when(first_visit)
    def _():
        ys_ref[...] = packed

    @pl.when(jnp.logical_not(first_visit))
    def _():
        rows = vblk_ref[v] * rb + lax.broadcasted_iota(I32, packed.shape, 0)
        mine = (rows >= vlo_ref[v]) & (rows < vhi_ref[v])
        ys_ref[...] = jnp.where(mine, packed, ys_ref[...])

    convert_chunks(nexp_ref[v], 1 - par, clo_ref[v], chi_ref[v])


def _weight_chunks(f):
    chunks = 16
    while f % (chunks * V7X_BF16_SUBLANES):
        chunks //= 2
    return chunks


def _expert_ffn(visits, xs, wg, wu, wd, *, rb):
    rows, half = xs.shape
    n_exp, d, f = wg.shape
    n_visits = visits[0].shape[0]
    chunks = _weight_chunks(f)
    vmem = (2 * 3 * d * f * 2 + 3 * 2 * (d // chunks) * f * 4 + 4 * rb * half * 4) / MIB
    block = lambda v, *tables: (tables[0][v], 0)
    grid_spec = pltpu.PrefetchScalarGridSpec(
        num_scalar_prefetch=len(visits),
        grid=(n_visits,),
        in_specs=[
            pl.BlockSpec((rb, half), block),
            pl.BlockSpec(memory_space=pl.ANY),
            pl.BlockSpec(memory_space=pl.ANY),
            pl.BlockSpec(memory_space=pl.ANY),
        ],
        out_specs=pl.BlockSpec((rb, half), block),
        scratch_shapes=[
            pltpu.VMEM((2, d, f), BF16), pltpu.VMEM((2, d, f), BF16), pltpu.VMEM((2, f, d), BF16),
            pltpu.VMEM((2, d // chunks, f), F32), pltpu.VMEM((2, d // chunks, f), F32),
            pltpu.VMEM((2, f // chunks, d), F32),
            pltpu.SemaphoreType.DMA((3, 2)),
        ],
    )
    return pl.pallas_call(
        functools.partial(_ffn_body, rb=rb, chunks=chunks),
        out_shape=jax.ShapeDtypeStruct((rows, half), U32),
        grid_spec=grid_spec,
        compiler_params=_compiler_params(("arbitrary",), vmem),
        name="expert_ffn",
    )(*visits, xs, wg, wu, wd)


def _visit_tables(counts, n_rows, rb, chunks):
    n_exp = counts.shape[0]
    n_blocks = n_rows // rb
    n_visits = n_blocks + n_exp
    ends = jnp.cumsum(counts)
    starts = ends - counts
    first_blk = starts // rb
    last_blk = jnp.maximum(ends - 1, starts) // rb
    per_exp = jnp.where(counts > 0, last_blk - first_blk + 1, 0)
    vis_end = jnp.cumsum(per_exp)
    vis_start = vis_end - per_exp
    total = vis_end[-1]
    v = jnp.arange(n_visits, dtype=I32)
    e = jnp.minimum(jnp.sum((v[:, None] >= vis_end[None, :]).astype(I32), axis=1), n_exp - 1)
    is_e = e[:, None] == jnp.arange(n_exp, dtype=I32)[None, :]
    of_e = lambda table: jnp.sum(jnp.where(is_e, table[None, :], 0), axis=1)
    blk = of_e(first_blk) + (v - of_e(vis_start))
    lo = jnp.maximum(of_e(starts), blk * rb)
    hi = jnp.minimum(of_e(ends), (blk + 1) * rb)
    used = v < total
    ids = jnp.arange(n_exp, dtype=I32)
    nonempty = counts > 0
    last_e = jnp.max(jnp.where(nonempty, ids, 0))
    slot_of = (jnp.cumsum(nonempty.astype(I32)) - 1) % 2
    later = (ids[None, :] > ids[:, None]) & nonempty[None, :]
    next_of = jnp.min(jnp.where(later, ids[None, :], n_exp), axis=1)
    step_in_e = v - of_e(vis_start)
    steps_of_e = jnp.maximum(of_e(per_exp), 1)
    has_next = used & (of_e(next_of) < n_exp)
    clo = jnp.where(has_next, chunks * step_in_e // steps_of_e, 0)
    chi = jnp.where(has_next, chunks * (step_in_e + 1) // steps_of_e, 0)
    nxt = jnp.where(has_next, of_e(next_of), last_e)
    par = jnp.where(used, of_e(slot_of), jnp.sum(jnp.where(ids == last_e, slot_of, 0)))
    blk = jnp.where(used, blk, n_blocks - 1)
    e = jnp.where(used, e, last_e)
    lo = jnp.where(used, lo, 0)
    hi = jnp.where(used, hi, 0)
    return tuple(t.astype(I32) for t in (blk, e, lo, hi, par, nxt, clo, chi))


def _combine_body(slot_ref, next_slot_ref, w_ref, h2_ref, ys_ref, g_ref, o_ref, ybuf, sems, *, tt, d, tiles):
    half = d // 2
    i = pl.program_id(0)
    cur = i % 2

    def start_tile(slots, buf):
        def start(g, carry):
            base = pl.multiple_of(g * ROW_GROUP, ROW_GROUP)
            for s in range(ROW_GROUP):
                for k in range(TOP_K):
                    pltpu.make_async_copy(ys_ref.at[pl.ds(slots[(base + s) * TOP_K + k], 1)],
                                          ybuf.at[buf, k, pl.ds(base + s, 1)],
                                          sems.at[buf]).start(priority=k % DMA_PRIORITIES)
            return carry
        lax.fori_loop(0, tt // ROW_GROUP, start, 0)

    @pl.when(i == 0)
    def _():
        start_tile(slot_ref, 0)

    def wait_tile(buf):
        for k in range(TOP_K):
            pltpu.make_async_copy(ys_ref.at[pl.ds(0, tt)], ybuf.at[buf, k], sems.at[buf]).wait()

    def combine_rows(buf, rows):
        acc_lo = h2_ref[rows, :half]
        acc_hi = h2_ref[rows, half:]
        for k in range(TOP_K):
            lo, hi = _unpack_halves(ybuf[buf, k, rows, :])
            wk = w_ref[rows, k:k + 1]
            acc_lo = acc_lo + wk * lo
            acc_hi = acc_hi + wk * hi
        ss = jnp.sum(acc_lo * acc_lo, axis=-1, keepdims=True) + jnp.sum(acc_hi * acc_hi, axis=-1, keepdims=True)
        inv = lax.rsqrt(ss / d + EPS)
        o_ref[rows, :half] = acc_lo * inv * g_ref[:, :half]
        o_ref[rows, half:] = acc_hi * inv * g_ref[:, half:]

    for buf in range(2):
        @pl.when(cur == buf)
        def _():
            wait_tile(buf)
            for grp in range(tt // ROW_GROUP):
                rows = slice(grp * ROW_GROUP, (grp + 1) * ROW_GROUP)
                combine_rows(buf, rows)
                for r in range(rows.start, rows.stop):
                    for k in range(TOP_K):
                        pltpu.make_async_copy(ys_ref.at[pl.ds(next_slot_ref[r * TOP_K + k], 1)],
                                              ybuf.at[1 - buf, k, pl.ds(r, 1)],
                                              sems.at[1 - buf]).start(priority=k % DMA_PRIORITIES)

            @pl.when(i == tiles - 1)
            def _():
                wait_tile(1 - buf)


def _moe_combine(slot_flat, w_tok, h2, ys, g):
    n, d = h2.shape
    half = d // 2
    tt = _tile(n, 128)
    tiles = n // tt
    vmem = (2 * TOP_K * tt * half * 4 + 4 * tt * d * 4) / MIB
    return pl.pallas_call(
        functools.partial(_combine_body, tt=tt, d=d, tiles=tiles),
        out_shape=jax.ShapeDtypeStruct((n, d), F32),
        grid=(tiles,),
        in_specs=[
            pl.BlockSpec((tt * TOP_K,), lambda i: (i,), memory_space=pltpu.SMEM),
            pl.BlockSpec((tt * TOP_K,), lambda i: (jnp.minimum(i + 1, tiles - 1),), memory_space=pltpu.SMEM),
            pl.BlockSpec((tt, TOP_K), lambda i: (i, 0)),
            pl.BlockSpec((tt, d), lambda i: (i, 0)),
            pl.BlockSpec(memory_space=pl.ANY),
            pl.BlockSpec((1, d), lambda i: (0, 0)),
        ],
        out_specs=pl.BlockSpec((tt, d), lambda i: (i, 0)),
        scratch_shapes=[pltpu.VMEM((2, TOP_K, tt, half), U32), pltpu.SemaphoreType.DMA((2,))],
        compiler_params=_compiler_params(("arbitrary",), vmem),
        name="moe_combine",
    )(slot_flat, slot_flat, w_tok, h2, ys, g)


def _layer(h, l, p, dims):
    b, s, d = h.shape
    n = b * s
    x2 = h.reshape(n, d)
    aw, kvw, cw = dims["attn"], dims["kv"], dims["conv"]
    _, k_off, v_off, cv_off, cg_off, ga_off, gc_off = dims["src_offsets"]

    proj = _in_projection(x2, p["attn_norm_g"][l].reshape(1, d), p["w_in"][l].astype(BF16))
    attn = _window_attention(proj, p["sink_logits"][l], seq=s, attn_width=aw, kv_width=kvw,
                             k_off=k_off, v_off=v_off)
    conv = _conformer_conv(proj, p["conv_dw_w"][l], p["conv_dw_b"][l], p["conv_ln_g"][l],
                           p["conv_ln_b"][l], seq=s, a_off=cv_off, b_off=cg_off)
    merged = _gated_merge(attn, conv, p["w_o_attn"][l].astype(BF16), p["w_o_conv"][l].astype(BF16),
                          proj, ga_off=ga_off, gc_off=gc_off)
    h1 = _out_projection(merged, p["w_out"][l].astype(BF16), x2)

    n_exp = p["w_router"].shape[-1]
    xp, eidx, rank, wts, cnt = _router(
        h1, p["ffn_norm_g"][l].reshape(1, d), p["w_router"][l].T, p["router_bias"][l].reshape(n_exp, 1))
    h2 = _shared_expert(xp, h1, p["w_sh_gate"][l].astype(BF16), p["w_sh_up"][l].astype(BF16),
                        p["w_sh_down"][l].astype(BF16))

    counts = cnt[:, 0].astype(I32)
    starts = jnp.cumsum(counts) - counts
    is_e = eidx[None, :, :] == jnp.arange(n_exp, dtype=I32)[:, None, None]
    slot = rank + jnp.sum(jnp.where(is_e, starts[:, None, None], 0), axis=0)
    slot_flat = slot.T.reshape(-1)
    rb = _tile(n * TOP_K, 256)
    visits = _visit_tables(counts, n * TOP_K, rb, _weight_chunks(p["w_exp_gate"].shape[-1]))

    xs = _moe_dispatch(slot_flat, xp)
    ys = _expert_ffn(visits, xs, p["w_exp_gate"][l], p["w_exp_up"][l], p["w_exp_down"][l], rb=rb)
    return slot_flat, wts.T, h2, ys


def kernel(x, attn_norm_g, w_in, sink_logits, w_o_attn, conv_dw_w, conv_dw_b, conv_ln_g, conv_ln_b,
           w_o_conv, w_out, ffn_norm_g, w_router, router_bias, w_exp_gate, w_exp_up, w_exp_down,
           w_sh_gate, w_sh_up, w_sh_down, final_norm_g):
    b, s, d = x.shape
    depth = w_in.shape[0]
    assert depth == 1, "the final RMSNorm is fused into the only layer's MoE combine"
    aw = w_o_attn.shape[1]
    cw = conv_dw_w.shape[2]
    kvw = (w_in.shape[2] - aw - 2 * cw - 2 * d) // 2
    assert s % Q_BLOCK == 0 and WINDOW <= Q_BLOCK
    src = [0, aw, aw + kvw, aw + 2 * kvw, aw + 2 * kvw + cw, aw + 2 * kvw + 2 * cw, aw + 2 * kvw + 2 * cw + d]
    dims = {"attn": aw, "kv": kvw, "conv": cw, "src_offsets": src}
    p = dict(attn_norm_g=attn_norm_g, w_in=w_in, sink_logits=sink_logits, w_o_attn=w_o_attn,
             conv_dw_w=conv_dw_w, conv_dw_b=conv_dw_b, conv_ln_g=conv_ln_g, conv_ln_b=conv_ln_b,
             w_o_conv=w_o_conv, w_out=w_out, ffn_norm_g=ffn_norm_g, w_router=w_router,
             router_bias=router_bias, w_exp_gate=w_exp_gate, w_exp_up=w_exp_up, w_exp_down=w_exp_down,
             w_sh_gate=w_sh_gate, w_sh_up=w_sh_up, w_sh_down=w_sh_down)
    slot_flat, w_tok, h2, ys = _layer(x, 0, p, dims)
    out = _moe_combine(slot_flat, w_tok, h2, ys, final_norm_g.reshape(1, d))
    return out.reshape(b, s, d)
```

```python
import functools

import jax
import jax.numpy as jnp
from jax import lax
from jax.experimental import pallas as pl
from jax.experimental.pallas import tpu as pltpu

F32, BF16, U32, I32 = jnp.float32, jnp.bfloat16, jnp.uint32, jnp.int32

EPS = 1e-6
WINDOW = 128
Q_BLOCK = 128
TOP_K = 8
N_GROUPS = 8
TOPK_GROUPS = 4
ROUTED_SCALE = 2.5

V7X_VMEM_BYTES = 64 * 1024 * 1024
V7X_BF16_SUBLANES = 16
MIB = 1024 * 1024
VMEM_CEILING_BYTES = V7X_VMEM_BYTES - 6 * MIB
VMEM_TEMPORARIES_MIB = 16
NORM_ROWS = 32
DMA_PRIORITIES = 2


def _compiler_params(semantics, window_mib):
    return pltpu.CompilerParams(
        dimension_semantics=semantics,
        vmem_limit_bytes=min(int((window_mib + VMEM_TEMPORARIES_MIB) * MIB), VMEM_CEILING_BYTES),
    )


def _tile(dim, target):
    t = min(dim, target)
    while dim % t:
        t //= 2
    return t


def _sigmoid(v):
    return 1.0 / (1.0 + jnp.exp(-v))


def _pack_halves(v_f32):
    bits = lax.bitcast_convert_type(v_f32, U32)
    h = v_f32.shape[1] // 2
    return (bits[:, :h] >> 16) | bits[:, h:]


def _unpack_halves(p_u32):
    lo = lax.bitcast_convert_type(p_u32 << 16, F32)
    hi = lax.bitcast_convert_type(p_u32 & jnp.uint32(0xFFFF0000), F32)
    return lo, hi


def _inproj_body(x_ref, g_ref, w_ref, o_ref, xn_ref):
    @pl.when(pl.program_id(1) == 0)
    def _():
        def norm_rows(c, carry):
            rows = pl.ds(pl.multiple_of(c * NORM_ROWS, NORM_ROWS), NORM_ROWS)
            x = x_ref[rows, :]
            ms = jnp.mean(x * x, axis=-1, keepdims=True)
            xn_ref[rows, :] = (x * lax.rsqrt(ms + EPS) * g_ref[...]).astype(BF16)
            return carry

        lax.fori_loop(0, x_ref.shape[0] // NORM_ROWS, norm_rows, 0, unroll=True)

    o_ref[...] = jnp.dot(xn_ref[...], w_ref[...], preferred_element_type=F32).astype(o_ref.dtype)


def _in_projection(x2, g, w):
    n, d = x2.shape
    wid = w.shape[1]
    tm, tn = _tile(n, 512), _tile(wid, 1024)
    vmem = (2 * tm * d * 4 + tm * d * 2 + 2 * d * tn * 2 + 2 * tm * tn * 2) / MIB
    return pl.pallas_call(
        _inproj_body,
        out_shape=jax.ShapeDtypeStruct((n, wid), BF16),
        grid=(n // tm, wid // tn),
        in_specs=[
            pl.BlockSpec((tm, d), lambda i, j: (i, 0)),
            pl.BlockSpec((1, d), lambda i, j: (0, 0)),
            pl.BlockSpec((d, tn), lambda i, j: (0, j)),
        ],
        out_specs=pl.BlockSpec((tm, tn), lambda i, j: (i, j)),
        scratch_shapes=[pltpu.VMEM((tm, d), BF16)],
        compiler_params=_compiler_params(("parallel", "arbitrary"), vmem),
        name="in_projection",
    )(x2, g, w)


ATTN_BLOCKS_PER_STEP = 2


def _attn_body(sink_ref, q_ref, kp_ref, kc_ref, kn_ref, vp_ref, vc_ref, vn_ref, o_ref, s_ref, p_ref, *,
               blocks_per_seq, n_heads, n_kv_heads, head_dim, qb):
    tq = Q_BLOCK
    k = jnp.concatenate([kp_ref[...], kc_ref[...], kn_ref[...]], axis=0)
    v = jnp.concatenate([vp_ref[...], vc_ref[...], vn_ref[...]], axis=0)
    qi = lax.broadcasted_iota(I32, (tq, 3 * tq), 0)
    ki = lax.broadcasted_iota(I32, (tq, 3 * tq), 1)
    rel = ki - tq - qi
    in_window = jnp.abs(rel) <= WINDOW
    dist = jnp.abs(rel).astype(F32)
    valid = []
    for b in range(qb):
        nb = (pl.program_id(0) * qb + b) % blocks_per_seq
        valid.append(in_window & ((ki >= tq) | (nb > 0)) & ((ki < 2 * tq) | (nb < blocks_per_seq - 1)))
    group = n_heads // n_kv_heads
    scale = head_dim ** -0.5
    pairs = [(h, b) for h in range(n_heads) for b in range(qb)]
    for idx, (h, b) in enumerate(pairs):
        j = h // group
        qh = q_ref[b * tq:(b + 1) * tq, h * head_dim:(h + 1) * head_dim]
        kj = k[b * tq:(b + 3) * tq, j * head_dim:(j + 1) * head_dim]
        s_ref[idx] = lax.dot_general(qh, kj, (((1,), (1,)), ((), ())), preferred_element_type=F32)
    for idx, (h, b) in enumerate(pairs):
        slope = 2.0 ** (-8.0 * (h + 1) / n_heads)
        sink = sink_ref[h]
        s = jnp.where(valid[b], s_ref[idx] * scale - slope * dist, -jnp.inf)
        m = jnp.maximum(jnp.max(s, axis=-1, keepdims=True), sink)
        e = jnp.exp(s - m)
        denom = jnp.sum(e, axis=-1, keepdims=True) + jnp.exp(sink - m)
        p_ref[idx] = (e * (1.0 / denom)).astype(BF16)
    for idx, (h, b) in enumerate(pairs):
        j = h // group
        vj = v[b * tq:(b + 3) * tq, j * head_dim:(j + 1) * head_dim]
        pv = jnp.dot(p_ref[idx], vj, preferred_element_type=F32)
        o_ref[b * tq:(b + 1) * tq, h * head_dim:(h + 1) * head_dim] = pv.astype(o_ref.dtype)


def _window_attention(proj, sink, *, seq, attn_width, kv_width, k_off, v_off):
    n = proj.shape[0]
    n_heads = sink.shape[0]
    head_dim = attn_width // n_heads
    n_kv_heads = kv_width // head_dim
    tq = Q_BLOCK
    nblk = n // tq
    qb = ATTN_BLOCKS_PER_STEP if nblk % ATTN_BLOCKS_PER_STEP == 0 else 1
    kb, vb = k_off // kv_width, v_off // kv_width
    body = functools.partial(_attn_body, blocks_per_seq=seq // tq, n_heads=n_heads,
                             n_kv_heads=n_kv_heads, head_dim=head_dim, qb=qb)

    def band(col):
        return [
            pl.BlockSpec((tq, kv_width), lambda i: (jnp.maximum(i * qb - 1, 0), col)),
            pl.BlockSpec((qb * tq, kv_width), lambda i: (i, col)),
            pl.BlockSpec((tq, kv_width), lambda i: (jnp.minimum((i + 1) * qb, nblk - 1), col)),
        ]

    return pl.pallas_call(
        body,
        out_shape=jax.ShapeDtypeStruct((n, attn_width), BF16),
        grid=(nblk // qb,),
        in_specs=[pl.BlockSpec(memory_space=pltpu.SMEM),
                  pl.BlockSpec((qb * tq, attn_width), lambda i: (i, 0))] + band(kb) + band(vb),
        out_specs=pl.BlockSpec((qb * tq, attn_width), lambda i: (i, 0)),
        scratch_shapes=[pltpu.VMEM((n_heads * qb, tq, 3 * tq), F32), pltpu.VMEM((n_heads * qb, tq, 3 * tq), BF16)],
        compiler_params=_compiler_params(("parallel",), 8 + n_heads * qb * tq * 3 * tq * 6 / MIB),
        name="window_attention",
    )(sink, proj, proj, proj, proj, proj, proj, proj)


CONV_ROWS = 64
CONV_LANES = 256


def _conv_body(*refs, tiles_per_seq, tt, c, cb, ksize):
    pieces = c // cb
    a_refs, b_refs = refs[:3 * pieces], refs[3 * pieces:6 * pieces]
    w_ref, bias_ref, lg_ref, lb_ref, o_ref, u_ref, shift_ref, y_ref = refs[6 * pieces:]
    ti = pl.program_id(0) % tiles_per_seq
    halo = V7X_BF16_SUBLANES

    def glu(a_ref, b_ref):
        return a_ref[...].astype(F32) * _sigmoid(b_ref[...].astype(F32))

    for p in range(pieces):
        ap_ref, ac_ref, an_ref = a_refs[3 * p:3 * p + 3]
        bp_ref, bc_ref, bn_ref = b_refs[3 * p:3 * p + 3]
        cols = slice(p * cb, (p + 1) * cb)
        u_ref[0:halo, cols] = jnp.where(ti > 0, glu(ap_ref, bp_ref), 0.0)
        u_ref[halo:halo + tt, cols] = glu(ac_ref, bc_ref)
        u_ref[halo + tt:, cols] = jnp.where(ti < tiles_per_seq - 1, glu(an_ref, bn_ref), 0.0)

    first = halo - ksize // 2
    n_rows = tt + 2 * halo
    cl, rr = min(c, CONV_LANES), min(tt, CONV_ROWS)
    for c0 in range(0, c, cl):
        lanes = slice(c0, c0 + cl)
        tile = u_ref[:, lanes]
        for phase in range(1, 8):
            shift_ref[phase - 1, :, lanes] = pltpu.roll(tile, n_rows - phase, 0)
        for r0 in range(0, tt, rr):
            acc = jnp.zeros((rr, cl), F32)
            for j in range(ksize):
                phase, a = (j + first) % 8, r0 + (j + first) // 8 * 8
                assert a + rr <= n_rows - 8
                src = u_ref[a:a + rr, lanes] if phase == 0 else shift_ref[phase - 1, a:a + rr, lanes]
                acc = acc + src * w_ref[j:j + 1, lanes]
            y_ref[r0:r0 + rr, lanes] = acc + bias_ref[:, lanes]

    y = y_ref[...]
    mu = jnp.mean(y, axis=-1, keepdims=True)
    yc = y - mu
    var = jnp.mean(yc * yc, axis=-1, keepdims=True)
    z = yc * lax.rsqrt(var + EPS) * lg_ref[...] + lb_ref[...]
    o_ref[...] = (z * _sigmoid(z)).astype(o_ref.dtype)


def _conformer_conv(proj, w_dw, b_dw, ln_g, ln_b, *, seq, a_off, b_off):
    n = proj.shape[0]
    ksize, c = w_dw.shape
    halo = V7X_BF16_SUBLANES
    assert ksize // 2 <= halo
    tt = _tile(seq, 256)
    cb = c
    while a_off % cb or b_off % cb:
        cb //= 2
    assert cb % 128 == 0 and c % cb == 0
    pieces = c // cb
    hb = tt // halo
    nhalo = n // halo
    body = functools.partial(_conv_body, tiles_per_seq=seq // tt, tt=tt, c=c, cb=cb, ksize=ksize)

    def band(col):
        return [
            pl.BlockSpec((halo, cb), lambda i: (jnp.maximum(i * hb - 1, 0), col)),
            pl.BlockSpec((tt, cb), lambda i: (i, col)),
            pl.BlockSpec((halo, cb), lambda i: (jnp.minimum((i + 1) * hb, nhalo - 1), col)),
        ]

    bands = []
    for off in (a_off, b_off):
        for p in range(pieces):
            bands += band(off // cb + p)
    row = lambda a: a.reshape(1, c)
    vec = pl.BlockSpec((1, c), lambda i: (0, 0))
    return pl.pallas_call(
        body,
        out_shape=jax.ShapeDtypeStruct((n, c), BF16),
        grid=(n // tt,),
        in_specs=bands + [pl.BlockSpec((ksize, c), lambda i: (0, 0)), vec, vec, vec],
        out_specs=pl.BlockSpec((tt, c), lambda i: (i, 0)),
        scratch_shapes=[pltpu.VMEM((tt + 2 * halo, c), F32), pltpu.VMEM((7, tt + 2 * halo, c), F32),
                        pltpu.VMEM((tt, c), F32)],
        compiler_params=_compiler_params(("parallel",), (8 * (tt + 2 * halo) + 5 * tt) * c * 4 / MIB),
        name="conformer_conv",
    )(*([proj] * (6 * pieces)), w_dw, row(b_dw), row(ln_g), row(ln_b))


def _merge_body(attn_ref, conv_ref, woa_ref, woc_ref, ga_ref, gc_ref, o_ref):
    ad = jnp.dot(attn_ref[...], woa_ref[...], preferred_element_type=F32)
    cd = jnp.dot(conv_ref[...], woc_ref[...], preferred_element_type=F32)
    merged = _sigmoid(ga_ref[...].astype(F32)) * ad + _sigmoid(gc_ref[...].astype(F32)) * cd
    o_ref[...] = merged.astype(o_ref.dtype)


def _gated_merge(attn, conv, w_oa, w_oc, proj, *, ga_off, gc_off):
    n, aw = attn.shape
    cw = conv.shape[1]
    d = w_oa.shape[1]
    tm, tn = _tile(n, 1024), _tile(d, 512)
    gab, gcb = ga_off // tn, gc_off // tn
    vmem = (2 * tm * (aw + cw) * 2 + 2 * (aw + cw) * tn * 2 + 6 * tm * tn * 2) / MIB
    return pl.pallas_call(
        _merge_body,
        out_shape=jax.ShapeDtypeStruct((n, d), BF16),
        grid=(n // tm, d // tn),
        in_specs=[
            pl.BlockSpec((tm, aw), lambda i, j: (i, 0)),
            pl.BlockSpec((tm, cw), lambda i, j: (i, 0)),
            pl.BlockSpec((aw, tn), lambda i, j: (0, j)),
            pl.BlockSpec((cw, tn), lambda i, j: (0, j)),
            pl.BlockSpec((tm, tn), lambda i, j: (i, gab + j)),
            pl.BlockSpec((tm, tn), lambda i, j: (i, gcb + j)),
        ],
        out_specs=pl.BlockSpec((tm, tn), lambda i, j: (i, j)),
        compiler_params=_compiler_params(("parallel", "parallel"), vmem),
        name="gated_merge",
    )(attn, conv, w_oa, w_oc, proj, proj)


def _outproj_body(m_ref, w_ref, x_ref, o_ref):
    o_ref[...] = x_ref[...] + jnp.dot(m_ref[...], w_ref[...], preferred_element_type=F32)


def _out_projection(merged, w_out, x2):
    n, d = merged.shape
    tm, tn = _tile(n, 1024), _tile(d, 512)
    vmem = (2 * tm * d * 2 + 2 * d * tn * 2 + 4 * tm * tn * 4) / MIB
    return pl.pallas_call(
        _outproj_body,
        out_shape=jax.ShapeDtypeStruct((n, d), F32),
        grid=(n // tm, d // tn),
        in_specs=[
            pl.BlockSpec((tm, d), lambda i, j: (i, 0)),
            pl.BlockSpec((d, tn), lambda i, j: (0, j)),
            pl.BlockSpec((tm, tn), lambda i, j: (i, j)),
        ],
        out_specs=pl.BlockSpec((tm, tn), lambda i, j: (i, j)),
        compiler_params=_compiler_params(("parallel", "parallel"), vmem),
        name="out_projection",
    )(merged, w_out, x2)


def _router_body(h_ref, g_ref, whi_ref, wlo_ref, rb_ref,
                 xp_ref, eidx_ref, rank_ref, wt_ref, cnt_ref, carry_ref, hhi_ref, hlo_ref, *, n_exp, tm):
    @pl.when(pl.program_id(0) == 0)
    def _():
        carry_ref[...] = jnp.zeros_like(carry_ref)

    def norm_rows(c, carry):
        rows = pl.ds(pl.multiple_of(c * NORM_ROWS, NORM_ROWS), NORM_ROWS)
        h = h_ref[rows, :]
        ms = jnp.mean(h * h, axis=-1, keepdims=True)
        hn = h * lax.rsqrt(ms + EPS) * g_ref[...]
        hi = hn.astype(BF16)
        hhi_ref[rows, :] = hi
        hlo_ref[rows, :] = (hn - hi.astype(F32)).astype(BF16)
        xp_ref[rows, :] = _pack_halves(hi.astype(F32))
        return carry

    lax.fori_loop(0, tm // NORM_ROWS, norm_rows, 0, unroll=True)

    nt = (((1,), (1,)), ((), ()))
    logits = (lax.dot_general(whi_ref[...], hhi_ref[...], nt, preferred_element_type=F32)
              + lax.dot_general(whi_ref[...], hlo_ref[...], nt, preferred_element_type=F32)
              + lax.dot_general(wlo_ref[...], hhi_ref[...], nt, preferred_element_type=F32)
              + lax.dot_general(wlo_ref[...], hlo_ref[...], nt, preferred_element_type=F32))
    score = _sigmoid(logits)
    choice = score + rb_ref[...]

    gsz = n_exp // N_GROUPS
    sub = lax.broadcasted_iota(I32, (gsz, tm), 0)
    groups, gscore = [], []
    for g in range(N_GROUPS):
        grp = choice[g * gsz:(g + 1) * gsz, :]
        m1 = jnp.max(grp, axis=0, keepdims=True)
        first = jnp.min(jnp.where(grp == m1, sub, gsz), axis=0, keepdims=True)
        m2 = jnp.max(jnp.where(sub == first, -jnp.inf, grp), axis=0, keepdims=True)
        groups.append(grp)
        gscore.append(m1 + m2)
    kept = []
    for g in range(N_GROUPS):
        beaten_by = jnp.zeros((1, tm), I32)
        for o in range(N_GROUPS):
            if o != g:
                wins = (gscore[o] >= gscore[g]) if o < g else (gscore[o] > gscore[g])
                beaten_by = beaten_by + wins.astype(I32)
        kept.append(jnp.where(beaten_by < TOPK_GROUPS, groups[g], -jnp.inf))
    cand = jnp.concatenate(kept, axis=0)

    eiota = lax.broadcasted_iota(I32, (n_exp, tm), 0)
    msel = jnp.zeros((n_exp, tm), F32)
    sel_idx, sel_score = [], []
    for _ in range(TOP_K):
        m = jnp.max(cand, axis=0, keepdims=True)
        first = jnp.min(jnp.where(cand == m, eiota, n_exp), axis=0, keepdims=True)
        hit = eiota == first
        sel_idx.append(first)
        sel_score.append(jnp.sum(jnp.where(hit, score, 0.0), axis=0, keepdims=True))
        msel = msel + hit.astype(F32)
        cand = jnp.where(hit, -jnp.inf, cand)

    rows = lax.broadcasted_iota(I32, (tm, tm), 0)
    cols = lax.broadcasted_iota(I32, (tm, tm), 1)
    earlier = (rows < cols).astype(BF16)
    before = jnp.dot(msel.astype(BF16), earlier, preferred_element_type=F32)
    rank_full = carry_ref[:, 0:1] + before
    total = carry_ref[...] + jnp.sum(msel, axis=1, keepdims=True)
    carry_ref[...] = total
    cnt_ref[...] = total

    wsum = sel_score[0]
    for k in range(1, TOP_K):
        wsum = wsum + sel_score[k]
    ranks = [jnp.sum(jnp.where(eiota == sel_idx[k], rank_full, 0.0), axis=0, keepdims=True)
             for k in range(TOP_K)]
    eidx_ref[...] = jnp.concatenate(sel_idx, axis=0)
    rank_ref[...] = jnp.concatenate(ranks, axis=0).astype(I32)
    wt_ref[...] = jnp.concatenate([sc / wsum * ROUTED_SCALE for sc in sel_score], axis=0)


def _const_spec(shape):
    return pl.BlockSpec(shape, lambda i: (0,) * len(shape), pipeline_mode=pl.Buffered(1))


def _router(h1, g, w_router_t, router_bias):
    n, d = h1.shape
    n_exp = w_router_t.shape[0]
    tm = _tile(n, 256)
    body = functools.partial(_router_body, n_exp=n_exp, tm=tm)
    w_hi = w_router_t.astype(BF16)
    w_lo = (w_router_t - w_hi.astype(F32)).astype(BF16)
    vmem = (2 * tm * d * 4 + 2 * tm * d * 2 + n_exp * d * 4 + tm * d * 4) / MIB
    return pl.pallas_call(
        body,
        out_shape=(
            jax.ShapeDtypeStruct((n, d // 2), U32),
            jax.ShapeDtypeStruct((TOP_K, n), I32),
            jax.ShapeDtypeStruct((TOP_K, n), I32),
            jax.ShapeDtypeStruct((TOP_K, n), F32),
            jax.ShapeDtypeStruct((n_exp, 128), F32),
        ),
        grid=(n // tm,),
        in_specs=[
            pl.BlockSpec((tm, d), lambda i: (i, 0)),
            _const_spec((1, d)),
            _const_spec((n_exp, d)),
            _const_spec((n_exp, d)),
            _const_spec((n_exp, 1)),
        ],
        out_specs=(
            pl.BlockSpec((tm, d // 2), lambda i: (i, 0)),
            pl.BlockSpec((TOP_K, tm), lambda i: (0, i)),
            pl.BlockSpec((TOP_K, tm), lambda i: (0, i)),
            pl.BlockSpec((TOP_K, tm), lambda i: (0, i)),
            pl.BlockSpec((n_exp, 128), lambda i: (0, 0)),
        ),
        scratch_shapes=[pltpu.VMEM((n_exp, 128), F32), pltpu.VMEM((tm, d), BF16), pltpu.VMEM((tm, d), BF16)],
        compiler_params=_compiler_params(("arbitrary",), vmem),
        name="router",
    )(h1, g, w_hi, w_lo, router_bias)


def _swiglu(x_packed, wg, wu, wd):
    lo, hi = _unpack_halves(x_packed)
    x = jnp.concatenate([lo.astype(BF16), hi.astype(BF16)], axis=1)
    gate = jnp.dot(x, wg, preferred_element_type=F32)
    up = jnp.dot(x, wu, preferred_element_type=F32)
    act = (gate * _sigmoid(gate) * up).astype(BF16)
    return jnp.dot(act, wd, preferred_element_type=F32)


def _shared_body(xp_ref, h1_ref, wg_ref, wu_ref, wd_ref, h2_ref):
    h2_ref[...] = h1_ref[...] + _swiglu(xp_ref[...], wg_ref[...], wu_ref[...], wd_ref[...])


def _shared_expert(xp, h1, wg, wu, wd):
    n, d = h1.shape
    f = wg.shape[1]
    tm = _tile(n, 256)
    vmem = (2 * tm * d * 2 + 4 * tm * d * 4 + 3 * d * f * 2) / MIB
    return pl.pallas_call(
        _shared_body,
        out_shape=jax.ShapeDtypeStruct((n, d), F32),
        grid=(n // tm,),
        in_specs=[
            pl.BlockSpec((tm, d // 2), lambda i: (i, 0)),
            pl.BlockSpec((tm, d), lambda i: (i, 0)),
            _const_spec((d, f)),
            _const_spec((d, f)),
            _const_spec((f, d)),
        ],
        out_specs=pl.BlockSpec((tm, d), lambda i: (i, 0)),
        compiler_params=_compiler_params(("parallel",), vmem),
        name="shared_expert",
    )(xp, h1, wg, wu, wd)


ROW_GROUP = 8


def _dispatch_body(slot_ref, xp_ref, xs_ref, sem, *, tt):
    def start(g, carry):
        base = pl.multiple_of(g * ROW_GROUP, ROW_GROUP)
        for s in range(ROW_GROUP):
            for k in range(TOP_K):
                pltpu.make_async_copy(xp_ref.at[pl.ds(base + s, 1)],
                                      xs_ref.at[pl.ds(slot_ref[(base + s) * TOP_K + k], 1)],
                                      sem).start(priority=k % DMA_PRIORITIES)
        return carry

    lax.fori_loop(0, tt // ROW_GROUP, start, 0)
    for _ in range(TOP_K):
        pltpu.make_async_copy(xp_ref, xs_ref.at[pl.ds(0, tt)], sem).wait()


def _moe_dispatch(slot_flat, xp):
    n, half = xp.shape
    tt = _tile(n, 256)
    return pl.pallas_call(
        functools.partial(_dispatch_body, tt=tt),
        out_shape=jax.ShapeDtypeStruct((n * TOP_K, half), U32),
        grid=(n // tt,),
        in_specs=[
            pl.BlockSpec((tt * TOP_K,), lambda i: (i,), memory_space=pltpu.SMEM),
            pl.BlockSpec((tt, half), lambda i: (i, 0)),
        ],
        out_specs=pl.BlockSpec(memory_space=pl.ANY),
        scratch_shapes=[pltpu.SemaphoreType.DMA(())],
        compiler_params=_compiler_params(("arbitrary",), 2 * tt * half * 4 / MIB),
        name="moe_dispatch",
    )(slot_flat, xp)


def _ffn_body(vblk_ref, vexp_ref, vlo_ref, vhi_ref, vpar_ref, nexp_ref, clo_ref, chi_ref,
              xs_ref, wg_hbm, wu_hbm, wd_hbm, ys_ref,
              wg_ref, wu_ref, wd_ref, sg_ref, su_ref, sd_ref, sems, *, rb, chunks):
    v = pl.program_id(0)
    mats = ((wg_hbm, wg_ref, sg_ref), (wu_hbm, wu_ref, su_ref), (wd_hbm, wd_ref, sd_ref))

    def chunk_copy(m, e, c, stage):
        hbm, _, stage_ref = mats[m]
        rows = stage_ref.shape[1]
        return pltpu.make_async_copy(hbm.at[e, pl.ds(c * rows, rows)], stage_ref.at[stage], sems.at[m, stage])

    def start_chunks(e, lo, hi):
        for m in range(3):
            for ahead in range(2):
                @pl.when(lo + ahead < hi)
                def _():
                    chunk_copy(m, e, lo + ahead, ahead).start()

    def convert_chunks(e, par, lo, hi):
        def convert(c, carry):
            stage = (c - lo) % 2
            for m in range(3):
                _, dst_ref, stage_ref = mats[m]
                rows = stage_ref.shape[1]
                chunk_copy(m, e, c, stage).wait()
                dst_ref[par, pl.ds(pl.multiple_of(c * rows, rows), rows), :] = stage_ref[stage].astype(BF16)

                @pl.when(c + 2 < hi)
                def _():
                    chunk_copy(m, e, c + 2, stage).start()
            return carry

        lax.fori_loop(lo, hi, convert, 0)

    @pl.when(v == 0)
    def _():
        start_chunks(vexp_ref[0], 0, chunks)
        convert_chunks(vexp_ref[0], vpar_ref[0], 0, chunks)

    start_chunks(nexp_ref[v], clo_ref[v], chi_ref[v])
    par = vpar_ref[v]
    y = _swiglu(xs_ref[...], wg_ref[par], wu_ref[par], wd_ref[par])
    packed = _pack_halves(y.astype(BF16).astype(F32))
    first_visit = (v == 0) | (vblk_ref[v] != vblk_ref[jnp.maximum(v - 1, 0)])

    @pl.when(first_visit)
    def _():
        ys_ref[...] = packed

    @pl.when(jnp.logical_not(first_visit))
    def _():
        rows = vblk_ref[v] * rb + lax.broadcasted_iota(I32, packed.shape, 0)
        mine = (rows >= vlo_ref[v]) & (rows < vhi_ref[v])
        ys_ref[...] = jnp.where(mine, packed, ys_ref[...])

    convert_chunks(nexp_ref[v], 1 - par, clo_ref[v], chi_ref[v])


def _weight_chunks(f):
    chunks = 16
    while f % (chunks * V7X_BF16_SUBLANES):
        chunks //= 2
    return chunks


def _expert_ffn(visits, xs, wg, wu, wd, *, rb):
    rows, half = xs.shape
    n_exp, d, f = wg.shape
    n_visits = visits[0].shape[0]
    chunks = _weight_chunks(f)
    vmem = (2 * 3 * d * f * 2 + 3 * 2 * (d // chunks) * f * 4 + 4 * rb * half * 4) / MIB
    block = lambda v, *tables: (tables[0][v], 0)
    grid_spec = pltpu.PrefetchScalarGridSpec(
        num_scalar_prefetch=len(visits),
        grid=(n_visits,),
        in_specs=[
            pl.BlockSpec((rb, half), block),
            pl.BlockSpec(memory_space=pl.ANY),
            pl.BlockSpec(memory_space=pl.ANY),
            pl.BlockSpec(memory_space=pl.ANY),
        ],
        out_specs=pl.BlockSpec((rb, half), block),
        scratch_shapes=[
            pltpu.VMEM((2, d, f), BF16), pltpu.VMEM((2, d, f), BF16), pltpu.VMEM((2, f, d), BF16),
            pltpu.VMEM((2, d // chunks, f), F32), pltpu.VMEM((2, d // chunks, f), F32),
            pltpu.VMEM((2, f // chunks, d), F32),
            pltpu.SemaphoreType.DMA((3, 2)),
        ],
    )
    return pl.pallas_call(
        functools.partial(_ffn_body, rb=rb, chunks=chunks),
        out_shape=jax.ShapeDtypeStruct((rows, half), U32),
        grid_spec=grid_spec,
        compiler_params=_compiler_params(("arbitrary",), vmem),
        name="expert_ffn",
    )(*visits, xs, wg, wu, wd)


def _visit_tables(counts, n_rows, rb, chunks):
    n_exp = counts.shape[0]
    n_blocks = n_rows // rb
    n_visits = n_blocks + n_exp
    ends = jnp.cumsum(counts)
    starts = ends - counts
    first_blk = starts // rb
    last_blk = jnp.maximum(ends - 1, starts) // rb
    per_exp = jnp.where(counts > 0, last_blk - first_blk + 1, 0)
    vis_end = jnp.cumsum(per_exp)
    vis_start = vis_end - per_exp
    total = vis_end[-1]
    v = jnp.arange(n_visits, dtype=I32)
    e = jnp.minimum(jnp.sum((v[:, None] >= vis_end[None, :]).astype(I32), axis=1), n_exp - 1)
    is_e = e[:, None] == jnp.arange(n_exp, dtype=I32)[None, :]
    of_e = lambda table: jnp.sum(jnp.where(is_e, table[None, :], 0), axis=1)
    blk = of_e(first_blk) + (v - of_e(vis_start))
    lo = jnp.maximum(of_e(starts), blk * rb)
    hi = jnp.minimum(of_e(ends), (blk + 1) * rb)
    used = v < total
    ids = jnp.arange(n_exp, dtype=I32)
    nonempty = counts > 0
    last_e = jnp.max(jnp.where(nonempty, ids, 0))
    slot_of = (jnp.cumsum(nonempty.astype(I32)) - 1) % 2
    later = (ids[None, :] > ids[:, None]) & nonempty[None, :]
    next_of = jnp.min(jnp.where(later, ids[None, :], n_exp), axis=1)
    step_in_e = v - of_e(vis_start)
    steps_of_e = jnp.maximum(of_e(per_exp), 1)
    has_next = used & (of_e(next_of) < n_exp)
    clo = jnp.where(has_next, chunks * step_in_e // steps_of_e, 0)
    chi = jnp.where(has_next, chunks * (step_in_e + 1) // steps_of_e, 0)
    nxt = jnp.where(has_next, of_e(next_of), last_e)
    par = jnp.where(used, of_e(slot_of), jnp.sum(jnp.where(ids == last_e, slot_of, 0)))
    blk = jnp.where(used, blk, n_blocks - 1)
    e = jnp.where(used, e, last_e)
    lo = jnp.where(used, lo, 0)
    hi = jnp.where(used, hi, 0)
    return tuple(t.astype(I32) for t in (blk, e, lo, hi, par, nxt, clo, chi))


def _combine_body(slot_ref, next_slot_ref, w_ref, h2_ref, ys_ref, g_ref, o_ref, ybuf, sems, *, tt, d, tiles):
    half = d // 2
    i = pl.program_id(0)
    cur = i % 2

    def start_tile(slots, buf):
        def start(g, carry):
            base = pl.multiple_of(g * ROW_GROUP, ROW_GROUP)
            for s in range(ROW_GROUP):
                for k in range(TOP_K):
                    pltpu.make_async_copy(ys_ref.at[pl.ds(slots[(base + s) * TOP_K + k], 1)],
                                          ybuf.at[buf, k, pl.ds(base + s, 1)],
                                          sems.at[buf]).start(priority=k % DMA_PRIORITIES)
            return carry
        lax.fori_loop(0, tt // ROW_GROUP, start, 0)

    @pl.when(i == 0)
    def _():
        start_tile(slot_ref, 0)

    def wait_tile(buf):
        for k in range(TOP_K):
            pltpu.make_async_copy(ys_ref.at[pl.ds(0, tt)], ybuf.at[buf, k], sems.at[buf]).wait()

    def combine_rows(buf, rows):
        acc_lo = h2_ref[rows, :half]
        acc_hi = h2_ref[rows, half:]
        for k in range(TOP_K):
            lo, hi = _unpack_halves(ybuf[buf, k, rows, :])
            wk = w_ref[rows, k:k + 1]
            acc_lo = acc_lo + wk * lo
            acc_hi = acc_hi + wk * hi
        ss = jnp.sum(acc_lo * acc_lo, axis=-1, keepdims=True) + jnp.sum(acc_hi * acc_hi, axis=-1, keepdims=True)
        inv = lax.rsqrt(ss / d + EPS)
        o_ref[rows, :half] = acc_lo * inv * g_ref[:, :half]
        o_ref[rows, half:] = acc_hi * inv * g_ref[:, half:]

    for buf in range(2):
        @pl.when(cur == buf)
        def _():
            wait_tile(buf)
            for grp in range(tt // ROW_GROUP):
                rows = slice(grp * ROW_GROUP, (grp + 1) * ROW_GROUP)
                combine_rows(buf, rows)
                for r in range(rows.start, rows.stop):
                    for k in range(TOP_K):
                        pltpu.make_async_copy(ys_ref.at[pl.ds(next_slot_ref[r * TOP_K + k], 1)],
                                              ybuf.at[1 - buf, k, pl.ds(r, 1)],
                                              sems.at[1 - buf]).start(priority=k % DMA_PRIORITIES)

            @pl.when(i == tiles - 1)
            def _():
                wait_tile(1 - buf)


def _moe_combine(slot_flat, w_tok, h2, ys, g):
    n, d = h2.shape
    half = d // 2
    tt = _tile(n, 128)
    tiles = n // tt
    vmem = (2 * TOP_K * tt * half * 4 + 4 * tt * d * 4) / MIB
    return pl.pallas_call(
        functools.partial(_combine_body, tt=tt, d=d, tiles=tiles),
        out_shape=jax.ShapeDtypeStruct((n, d), F32),
        grid=(tiles,),
        in_specs=[
            pl.BlockSpec((tt * TOP_K,), lambda i: (i,), memory_space=pltpu.SMEM),
            pl.BlockSpec((tt * TOP_K,), lambda i: (jnp.minimum(i + 1, tiles - 1),), memory_space=pltpu.SMEM),
            pl.BlockSpec((tt, TOP_K), lambda i: (i, 0)),
            pl.BlockSpec((tt, d), lambda i: (i, 0)),
            pl.BlockSpec(memory_space=pl.ANY),
            pl.BlockSpec((1, d), lambda i: (0, 0)),
        ],
        out_specs=pl.BlockSpec((tt, d), lambda i: (i, 0)),
        scratch_shapes=[pltpu.VMEM((2, TOP_K, tt, half), U32), pltpu.SemaphoreType.DMA((2,))],
        compiler_params=_compiler_params(("arbitrary",), vmem),
        name="moe_combine",
    )(slot_flat, slot_flat, w_tok, h2, ys, g)


def _layer(h, l, p, dims):
    b, s, d = h.shape
    n = b * s
    x2 = h.reshape(n, d)
    aw, kvw, cw = dims["attn"], dims["kv"], dims["conv"]
    _, k_off, v_off, cv_off, cg_off, ga_off, gc_off = dims["src_offsets"]

    proj = _in_projection(x2, p["attn_norm_g"][l].reshape(1, d), p["w_in"][l].astype(BF16))
    attn = _window_attention(proj, p["sink_logits"][l], seq=s, attn_width=aw, kv_width=kvw,
                             k_off=k_off, v_off=v_off)
    conv = _conformer_conv(proj, p["conv_dw_w"][l], p["conv_dw_b"][l], p["conv_ln_g"][l],
                           p["conv_ln_b"][l], seq=s, a_off=cv_off, b_off=cg_off)
    merged = _gated_merge(attn, conv, p["w_o_attn"][l].astype(BF16), p["w_o_conv"][l].astype(BF16),
                          proj, ga_off=ga_off, gc_off=gc_off)
    h1 = _out_projection(merged, p["w_out"][l].astype(BF16), x2)

    n_exp = p["w_router"].shape[-1]
    xp, eidx, rank, wts, cnt = _router(
        h1, p["ffn_norm_g"][l].reshape(1, d), p["w_router"][l].T, p["router_bias"][l].reshape(n_exp, 1))
    h2 = _shared_expert(xp, h1, p["w_sh_gate"][l].astype(BF16), p["w_sh_up"][l].astype(BF16),
                        p["w_sh_down"][l].astype(BF16))

    counts = cnt[:, 0].astype(I32)
    starts = jnp.cumsum(counts) - counts
    is_e = eidx[None, :, :] == jnp.arange(n_exp, dtype=I32)[:, None, None]
    slot = rank + jnp.sum(jnp.where(is_e, starts[:, None, None], 0), axis=0)
    slot_flat = slot.T.reshape(-1)
    rb = _tile(n * TOP_K, 256)
    visits = _visit_tables(counts, n * TOP_K, rb, _weight_chunks(p["w_exp_gate"].shape[-1]))

    xs = _moe_dispatch(slot_flat, xp)
    ys = _expert_ffn(visits, xs, p["w_exp_gate"][l], p["w_exp_up"][l], p["w_exp_down"][l], rb=rb)
    return slot_flat, wts.T, h2, ys


def kernel(x, attn_norm_g, w_in, sink_logits, w_o_attn, conv_dw_w, conv_dw_b, conv_ln_g, conv_ln_b,
           w_o_conv, w_out, ffn_norm_g, w_router, router_bias, w_exp_gate, w_exp_up, w_exp_down,
           w_sh_gate, w_sh_up, w_sh_down, final_norm_g):
    b, s, d = x.shape
    depth = w_in.shape[0]
    assert depth == 1, "the final RMSNorm is fused into the only layer's MoE combine"
    aw = w_o_attn.shape[1]
    cw = conv_dw_w.shape[2]
    kvw = (w_in.shape[2] - aw - 2 * cw - 2 * d) // 2
    assert s % Q_BLOCK == 0 and WINDOW <= Q_BLOCK
    src = [0, aw, aw + kvw, aw + 2 * kvw, aw + 2 * kvw + cw, aw + 2 * kvw + 2 * cw, aw + 2 * kvw + 2 * cw + d]
    dims = {"attn": aw, "kv": kvw, "conv": cw, "src_offsets": src}
    p = dict(attn_norm_g=attn_norm_g, w_in=w_in, sink_logits=sink_logits, w_o_attn=w_o_attn,
             conv_dw_w=conv_dw_w, conv_dw_b=conv_dw_b, conv_ln_g=conv_ln_g, conv_ln_b=conv_ln_b,
             w_o_conv=w_o_conv, w_out=w_out, ffn_norm_g=ffn_norm_g, w_router=w_router,
             router_bias=router_bias, w_exp_gate=w_exp_gate, w_exp_up=w_exp_up, w_exp_down=w_exp_down,
             w_sh_gate=w_sh_gate, w_sh_up=w_sh_up, w_sh_down=w_sh_down)
    slot_flat, w_tok, h2, ys = _layer(x, 0, p, dims)
    out = _moe_combine(slot_flat, w_tok, h2, ys, final_norm_g.reshape(1, d))
    return out.reshape(b, s, d)
```

```python
import functools

import jax
import jax.numpy as jnp
from jax import lax
from jax.experimental import pallas as pl
from jax.experimental.pallas import tpu as pltpu

F32, BF16, U32, I32 = jnp.float32, jnp.bfloat16, jnp.uint32, jnp.int32

EPS = 1e-6
WINDOW = 128
Q_BLOCK = 128
TOP_K = 8
N_GROUPS = 8
TOPK_GROUPS = 4
ROUTED_SCALE = 2.5

V7X_VMEM_BYTES = 64 * 1024 * 1024
V7X_BF16_SUBLANES = 16
MIB = 1024 * 1024
VMEM_CEILING_BYTES = V7X_VMEM_BYTES - 6 * MIB
VMEM_TEMPORARIES_MIB = 16
NORM_ROWS = 32
DMA_PRIORITIES = 2


def _compiler_params(semantics, window_mib):
    return pltpu.CompilerParams(
        dimension_semantics=semantics,
        vmem_limit_bytes=min(int((window_mib + VMEM_TEMPORARIES_MIB) * MIB), VMEM_CEILING_BYTES),
    )


def _tile(dim, target):
    t = min(dim, target)
    while dim % t:
        t //= 2
    return t


def _sigmoid(v):
    return 1.0 / (1.0 + jnp.exp(-v))


def _pack_halves(v_f32):
    bits = lax.bitcast_convert_type(v_f32, U32)
    h = v_f32.shape[1] // 2
    return (bits[:, :h] >> 16) | bits[:, h:]


def _unpack_halves(p_u32):
    lo = lax.bitcast_convert_type(p_u32 << 16, F32)
    hi = lax.bitcast_convert_type(p_u32 & jnp.uint32(0xFFFF0000), F32)
    return lo, hi


def _inproj_body(x_ref, g_ref, w_ref, o_ref, xn_ref):
    @pl.when(pl.program_id(1) == 0)
    def _():
        def norm_rows(c, carry):
            rows = pl.ds(pl.multiple_of(c * NORM_ROWS, NORM_ROWS), NORM_ROWS)
            x = x_ref[rows, :]
            ms = jnp.mean(x * x, axis=-1, keepdims=True)
            xn_ref[rows, :] = (x * lax.rsqrt(ms + EPS) * g_ref[...]).astype(BF16)
            return carry

        lax.fori_loop(0, x_ref.shape[0] // NORM_ROWS, norm_rows, 0, unroll=True)

    o_ref[...] = jnp.dot(xn_ref[...], w_ref[...], preferred_element_type=F32).astype(o_ref.dtype)


def _in_projection(x2, g, w):
    n, d = x2.shape
    wid = w.shape[1]
    tm, tn = _tile(n, 512), _tile(wid, 1024)
    vmem = (2 * tm * d * 4 + tm * d * 2 + 2 * d * tn * 2 + 2 * tm * tn * 2) / MIB
    return pl.pallas_call(
        _inproj_body,
        out_shape=jax.ShapeDtypeStruct((n, wid), BF16),
        grid=(n // tm, wid // tn),
        in_specs=[
            pl.BlockSpec((tm, d), lambda i, j: (i, 0)),
            pl.BlockSpec((1, d), lambda i, j: (0, 0)),
            pl.BlockSpec((d, tn), lambda i, j: (0, j)),
        ],
        out_specs=pl.BlockSpec((tm, tn), lambda i, j: (i, j)),
        scratch_shapes=[pltpu.VMEM((tm, d), BF16)],
        compiler_params=_compiler_params(("parallel", "arbitrary"), vmem),
        name="in_projection",
    )(x2, g, w)


ATTN_BLOCKS_PER_STEP = 2


def _attn_body(sink_ref, q_ref, kp_ref, kc_ref, kn_ref, vp_ref, vc_ref, vn_ref, o_ref, s_ref, p_ref, *,
               blocks_per_seq, n_heads, n_kv_heads, head_dim, qb):
    tq = Q_BLOCK
    k = jnp.concatenate([kp_ref[...], kc_ref[...], kn_ref[...]], axis=0)
    v = jnp.concatenate([vp_ref[...], vc_ref[...], vn_ref[...]], axis=0)
    qi = lax.broadcasted_iota(I32, (tq, 3 * tq), 0)
    ki = lax.broadcasted_iota(I32, (tq, 3 * tq), 1)
    rel = ki - tq - qi
    in_window = jnp.abs(rel) <= WINDOW
    dist = jnp.abs(rel).astype(F32)
    valid = []
    for b in range(qb):
        nb = (pl.program_id(0) * qb + b) % blocks_per_seq
        valid.append(in_window & ((ki >= tq) | (nb > 0)) & ((ki < 2 * tq) | (nb < blocks_per_seq - 1)))
    group = n_heads // n_kv_heads
    scale = head_dim ** -0.5
    pairs = [(h, b) for h in range(n_heads) for b in range(qb)]
    for idx, (h, b) in enumerate(pairs):
        j = h // group
        qh = q_ref[b * tq:(b + 1) * tq, h * head_dim:(h + 1) * head_dim]
        kj = k[b * tq:(b + 3) * tq, j * head_dim:(j + 1) * head_dim]
        s_ref[idx] = lax.dot_general(qh, kj, (((1,), (1,)), ((), ())), preferred_element_type=F32)
    for idx, (h, b) in enumerate(pairs):
        slope = 2.0 ** (-8.0 * (h + 1) / n_heads)
        sink = sink_ref[h]
        s = jnp.where(valid[b], s_ref[idx] * scale - slope * dist, -jnp.inf)
        m = jnp.maximum(jnp.max(s, axis=-1, keepdims=True), sink)
        e = jnp.exp(s - m)
        denom = jnp.sum(e, axis=-1, keepdims=True) + jnp.exp(sink - m)
        p_ref[idx] = (e * (1.0 / denom)).astype(BF16)
    for idx, (h, b) in enumerate(pairs):
        j = h // group
        vj = v[b * tq:(b + 3) * tq, j * head_dim:(j + 1) * head_dim]
        pv = jnp.dot(p_ref[idx], vj, preferred_element_type=F32)
        o_ref[b * tq:(b + 1) * tq, h * head_dim:(h + 1) * head_dim] = pv.astype(o_ref.dtype)


def _window_attention(proj, sink, *, seq, attn_width, kv_width, k_off, v_off):
    n = proj.shape[0]
    n_heads = sink.shape[0]
    head_dim = attn_width // n_heads
    n_kv_heads = kv_width // head_dim
    tq = Q_BLOCK
    nblk = n // tq
    qb = ATTN_BLOCKS_PER_STEP if nblk % ATTN_BLOCKS_PER_STEP == 0 else 1
    kb, vb = k_off // kv_width, v_off // kv_width
    body = functools.partial(_attn_body, blocks_per_seq=seq // tq, n_heads=n_heads,
                             n_kv_heads=n_kv_heads, head_dim=head_dim, qb=qb)

    def band(col):
        return [
            pl.BlockSpec((tq, kv_width), lambda i: (jnp.maximum(i * qb - 1, 0), col)),
            pl.BlockSpec((qb * tq, kv_width), lambda i: (i, col)),
            pl.BlockSpec((tq, kv_width), lambda i: (jnp.minimum((i + 1) * qb, nblk - 1), col)),
        ]

    return pl.pallas_call(
        body,
        out_shape=jax.ShapeDtypeStruct((n, attn_width), BF16),
        grid=(nblk // qb,),
        in_specs=[pl.BlockSpec(memory_space=pltpu.SMEM),
                  pl.BlockSpec((qb * tq, attn_width), lambda i: (i, 0))] + band(kb) + band(vb),
        out_specs=pl.BlockSpec((qb * tq, attn_width), lambda i: (i, 0)),
        scratch_shapes=[pltpu.VMEM((n_heads * qb, tq, 3 * tq), F32), pltpu.VMEM((n_heads * qb, tq, 3 * tq), BF16)],
        compiler_params=_compiler_params(("parallel",), 8 + n_heads * qb * tq * 3 * tq * 6 / MIB),
        name="window_attention",
    )(sink, proj, proj, proj, proj, proj, proj, proj)


CONV_ROWS = 64
CONV_LANES = 256


def _conv_body(*refs, tiles_per_seq, tt, c, cb, ksize):
    pieces = c // cb
    a_refs, b_refs = refs[:3 * pieces], refs[3 * pieces:6 * pieces]
    w_ref, bias_ref, lg_ref, lb_ref, o_ref, u_ref, shift_ref, y_ref = refs[6 * pieces:]
    ti = pl.program_id(0) % tiles_per_seq
    halo = V7X_BF16_SUBLANES

    def glu(a_ref, b_ref):
        return a_ref[...].astype(F32) * _sigmoid(b_ref[...].astype(F32))

    for p in range(pieces):
        ap_ref, ac_ref, an_ref = a_refs[3 * p:3 * p + 3]
        bp_ref, bc_ref, bn_ref = b_refs[3 * p:3 * p + 3]
        cols = slice(p * cb, (p + 1) * cb)
        u_ref[0:halo, cols] = jnp.where(ti > 0, glu(ap_ref, bp_ref), 0.0)
        u_ref[halo:halo + tt, cols] = glu(ac_ref, bc_ref)
        u_ref[halo + tt:, cols] = jnp.where(ti < tiles_per_seq - 1, glu(an_ref, bn_ref), 0.0)

    first = halo - ksize // 2
    n_rows = tt + 2 * halo
    cl, rr = min(c, CONV_LANES), min(tt, CONV_ROWS)
    for c0 in range(0, c, cl):
        lanes = slice(c0, c0 + cl)
        tile = u_ref[:, lanes]
        for phase in range(1, 8):
            shift_ref[phase - 1, :, lanes] = pltpu.roll(tile, n_rows - phase, 0)
        for r0 in range(0, tt, rr):
            acc = jnp.zeros((rr, cl), F32)
            for j in range(ksize):
                phase, a = (j + first) % 8, r0 + (j + first) // 8 * 8
                assert a + rr <= n_rows - 8
                src = u_ref[a:a + rr, lanes] if phase == 0 else shift_ref[phase - 1, a:a + rr, lanes]
                acc = acc + src * w_ref[j:j + 1, lanes]
            y_ref[r0:r0 + rr, lanes] = acc + bias_ref[:, lanes]

    y = y_ref[...]
    mu = jnp.mean(y, axis=-1, keepdims=True)
    yc = y - mu
    var = jnp.mean(yc * yc, axis=-1, keepdims=True)
    z = yc * lax.rsqrt(var + EPS) * lg_ref[...] + lb_ref[...]
    o_ref[...] = (z * _sigmoid(z)).astype(o_ref.dtype)


def _conformer_conv(proj, w_dw, b_dw, ln_g, ln_b, *, seq, a_off, b_off):
    n = proj.shape[0]
    ksize, c = w_dw.shape
    halo = V7X_BF16_SUBLANES
    assert ksize // 2 <= halo
    tt = _tile(seq, 256)
    cb = c
    while a_off % cb or b_off % cb:
        cb //= 2
    assert cb % 128 == 0 and c % cb == 0
    pieces = c // cb
    hb = tt // halo
    nhalo = n // halo
    body = functools.partial(_conv_body, tiles_per_seq=seq // tt, tt=tt, c=c, cb=cb, ksize=ksize)

    def band(col):
        return [
            pl.BlockSpec((halo, cb), lambda i: (jnp.maximum(i * hb - 1, 0), col)),
            pl.BlockSpec((tt, cb), lambda i: (i, col)),
            pl.BlockSpec((halo, cb), lambda i: (jnp.minimum((i + 1) * hb, nhalo - 1), col)),
        ]

    bands = []
    for off in (a_off, b_off):
        for p in range(pieces):
            bands += band(off // cb + p)
    row = lambda a: a.reshape(1, c)
    vec = pl.BlockSpec((1, c), lambda i: (0, 0))
    return pl.pallas_call(
        body,
        out_shape=jax.ShapeDtypeStruct((n, c), BF16),
        grid=(n // tt,),
        in_specs=bands + [pl.BlockSpec((ksize, c), lambda i: (0, 0)), vec, vec, vec],
        out_specs=pl.BlockSpec((tt, c), lambda i: (i, 0)),
        scratch_shapes=[pltpu.VMEM((tt + 2 * halo, c), F32), pltpu.VMEM((7, tt + 2 * halo, c), F32),
                        pltpu.VMEM((tt, c), F32)],
        compiler_params=_compiler_params(("parallel",), (8 * (tt + 2 * halo) + 5 * tt) * c * 4 / MIB),
        name="conformer_conv",
    )(*([proj] * (6 * pieces)), w_dw, row(b_dw), row(ln_g), row(ln_b))


def _merge_body(attn_ref, conv_ref, woa_ref, woc_ref, ga_ref, gc_ref, o_ref):
    ad = jnp.dot(attn_ref[...], woa_ref[...], preferred_element_type=F32)
    cd = jnp.dot(conv_ref[...], woc_ref[...], preferred_element_type=F32)
    merged = _sigmoid(ga_ref[...].astype(F32)) * ad + _sigmoid(gc_ref[...].astype(F32)) * cd
    o_ref[...] = merged.astype(o_ref.dtype)


def _gated_merge(attn, conv, w_oa, w_oc, proj, *, ga_off, gc_off):
    n, aw = attn.shape
    cw = conv.shape[1]
    d = w_oa.shape[1]
    tm, tn = _tile(n, 1024), _tile(d, 512)
    gab, gcb = ga_off // tn, gc_off // tn
    vmem = (2 * tm * (aw + cw) * 2 + 2 * (aw + cw) * tn * 2 + 6 * tm * tn * 2) / MIB
    return pl.pallas_call(
        _merge_body,
        out_shape=jax.ShapeDtypeStruct((n, d), BF16),
        grid=(n // tm, d // tn),
        in_specs=[
            pl.BlockSpec((tm, aw), lambda i, j: (i, 0)),
            pl.BlockSpec((tm, cw), lambda i, j: (i, 0)),
            pl.BlockSpec((aw, tn), lambda i, j: (0, j)),
            pl.BlockSpec((cw, tn), lambda i, j: (0, j)),
            pl.BlockSpec((tm, tn), lambda i, j: (i, gab + j)),
            pl.BlockSpec((tm, tn), lambda i, j: (i, gcb + j)),
        ],
        out_specs=pl.BlockSpec((tm, tn), lambda i, j: (i, j)),
        compiler_params=_compiler_params(("parallel", "parallel"), vmem),
        name="gated_merge",
    )(attn, conv, w_oa, w_oc, proj, proj)


def _outproj_body(m_ref, w_ref, x_ref, o_ref):
    o_ref[...] = x_ref[...] + jnp.dot(m_ref[...], w_ref[...], preferred_element_type=F32)


def _out_projection(merged, w_out, x2):
    n, d = merged.shape
    tm, tn = _tile(n, 1024), _tile(d, 512)
    vmem = (2 * tm * d * 2 + 2 * d * tn * 2 + 4 * tm * tn * 4) / MIB
    return pl.pallas_call(
        _outproj_body,
        out_shape=jax.ShapeDtypeStruct((n, d), F32),
        grid=(n // tm, d // tn),
        in_specs=[
            pl.BlockSpec((tm, d), lambda i, j: (i, 0)),
            pl.BlockSpec((d, tn), lambda i, j: (0, j)),
            pl.BlockSpec((tm, tn), lambda i, j: (i, j)),
        ],
        out_specs=pl.BlockSpec((tm, tn), lambda i, j: (i, j)),
        compiler_params=_compiler_params(("parallel", "parallel"), vmem),
        name="out_projection",
    )(merged, w_out, x2)


def _router_body(h_ref, g_ref, whi_ref, wlo_ref, rb_ref,
                 xp_ref, eidx_ref, rank_ref, wt_ref, cnt_ref, carry_ref, hhi_ref, hlo_ref, *, n_exp, tm):
    @pl.when(pl.program_id(0) == 0)
    def _():
        carry_ref[...] = jnp.zeros_like(carry_ref)

    def norm_rows(c, carry):
        rows = pl.ds(pl.multiple_of(c * NORM_ROWS, NORM_ROWS), NORM_ROWS)
        h = h_ref[rows, :]
        ms = jnp.mean(h * h, axis=-1, keepdims=True)
        hn = h * lax.rsqrt(ms + EPS) * g_ref[...]
        hi = hn.astype(BF16)
        hhi_ref[rows, :] = hi
        hlo_ref[rows, :] = (hn - hi.astype(F32)).astype(BF16)
        xp_ref[rows, :] = _pack_halves(hi.astype(F32))
        return carry

    lax.fori_loop(0, tm // NORM_ROWS, norm_rows, 0, unroll=True)

    nt = (((1,), (1,)), ((), ()))
    logits = (lax.dot_general(whi_ref[...], hhi_ref[...], nt, preferred_element_type=F32)
              + lax.dot_general(whi_ref[...], hlo_ref[...], nt, preferred_element_type=F32)
              + lax.dot_general(wlo_ref[...], hhi_ref[...], nt, preferred_element_type=F32)
              + lax.dot_general(wlo_ref[...], hlo_ref[...], nt, preferred_element_type=F32))
    score = _sigmoid(logits)
    choice = score + rb_ref[...]

    gsz = n_exp // N_GROUPS
    sub = lax.broadcasted_iota(I32, (gsz, tm), 0)
    groups, gscore = [], []
    for g in range(N_GROUPS):
        grp = choice[g * gsz:(g + 1) * gsz, :]
        m1 = jnp.max(grp, axis=0, keepdims=True)
        first = jnp.min(jnp.where(grp == m1, sub, gsz), axis=0, keepdims=True)
        m2 = jnp.max(jnp.where(sub == first, -jnp.inf, grp), axis=0, keepdims=True)
        groups.append(grp)
        gscore.append(m1 + m2)
    kept = []
    for g in range(N_GROUPS):
        beaten_by = jnp.zeros((1, tm), I32)
        for o in range(N_GROUPS):
            if o != g:
                wins = (gscore[o] >= gscore[g]) if o < g else (gscore[o] > gscore[g])
                beaten_by = beaten_by + wins.astype(I32)
        kept.append(jnp.where(beaten_by < TOPK_GROUPS, groups[g], -jnp.inf))
    cand = jnp.concatenate(kept, axis=0)

    eiota = lax.broadcasted_iota(I32, (n_exp, tm), 0)
    msel = jnp.zeros((n_exp, tm), F32)
    sel_idx, sel_score = [], []
    for _ in range(TOP_K):
        m = jnp.max(cand, axis=0, keepdims=True)
        first = jnp.min(jnp.where(cand == m, eiota, n_exp), axis=0, keepdims=True)
        hit = eiota == first
        sel_idx.append(first)
        sel_score.append(jnp.sum(jnp.where(hit, score, 0.0), axis=0, keepdims=True))
        msel = msel + hit.astype(F32)
        cand = jnp.where(hit, -jnp.inf, cand)

    rows = lax.broadcasted_iota(I32, (tm, tm), 0)
    cols = lax.broadcasted_iota(I32, (tm, tm), 1)
    earlier = (rows < cols).astype(BF16)
    before = jnp.dot(msel.astype(BF16), earlier, preferred_element_type=F32)
    rank_full = carry_ref[:, 0:1] + before
    total = carry_ref[...] + jnp.sum(msel, axis=1, keepdims=True)
    carry_ref[...] = total
    cnt_ref[...] = total

    wsum = sel_score[0]
    for k in range(1, TOP_K):
        wsum = wsum + sel_score[k]
    ranks = [jnp.sum(jnp.where(eiota == sel_idx[k], rank_full, 0.0), axis=0, keepdims=True)
             for k in range(TOP_K)]
    eidx_ref[...] = jnp.concatenate(sel_idx, axis=0)
    rank_ref[...] = jnp.concatenate(ranks, axis=0).astype(I32)
    wt_ref[...] = jnp.concatenate([sc / wsum * ROUTED_SCALE for sc in sel_score], axis=0)


def _const_spec(shape):
    return pl.BlockSpec(shape, lambda i: (0,) * len(shape), pipeline_mode=pl.Buffered(1))


def _router(h1, g, w_router_t, router_bias):
    n, d = h1.shape
    n_exp = w_router_t.shape[0]
    tm = _tile(n, 256)
    body = functools.partial(_router_body, n_exp=n_exp, tm=tm)
    w_hi = w_router_t.astype(BF16)
    w_lo = (w_router_t - w_hi.astype(F32)).astype(BF16)
    vmem = (2 * tm * d * 4 + 2 * tm * d * 2 + n_exp * d * 4 + tm * d * 4) / MIB
    return pl.pallas_call(
        body,
        out_shape=(
            jax.ShapeDtypeStruct((n, d // 2), U32),
            jax.ShapeDtypeStruct((TOP_K, n), I32),
            jax.ShapeDtypeStruct((TOP_K, n), I32),
            jax.ShapeDtypeStruct((TOP_K, n), F32),
            jax.ShapeDtypeStruct((n_exp, 128), F32),
        ),
        grid=(n // tm,),
        in_specs=[
            pl.BlockSpec((tm, d), lambda i: (i, 0)),
            _const_spec((1, d)),
            _const_spec((n_exp, d)),
            _const_spec((n_exp, d)),
            _const_spec((n_exp, 1)),
        ],
        out_specs=(
            pl.BlockSpec((tm, d // 2), lambda i: (i, 0)),
            pl.BlockSpec((TOP_K, tm), lambda i: (0, i)),
            pl.BlockSpec((TOP_K, tm), lambda i: (0, i)),
            pl.BlockSpec((TOP_K, tm), lambda i: (0, i)),
            pl.BlockSpec((n_exp, 128), lambda i: (0, 0)),
        ),
        scratch_shapes=[pltpu.VMEM((n_exp, 128), F32), pltpu.VMEM((tm, d), BF16), pltpu.VMEM((tm, d), BF16)],
        compiler_params=_compiler_params(("arbitrary",), vmem),
        name="router",
    )(h1, g, w_hi, w_lo, router_bias)


def _swiglu(x_packed, wg, wu, wd):
    lo, hi = _unpack_halves(x_packed)
    x = jnp.concatenate([lo.astype(BF16), hi.astype(BF16)], axis=1)
    gate = jnp.dot(x, wg, preferred_element_type=F32)
    up = jnp.dot(x, wu, preferred_element_type=F32)
    act = (gate * _sigmoid(gate) * up).astype(BF16)
    return jnp.dot(act, wd, preferred_element_type=F32)


def _shared_body(xp_ref, h1_ref, wg_ref, wu_ref, wd_ref, h2_ref):
    h2_ref[...] = h1_ref[...] + _swiglu(xp_ref[...], wg_ref[...], wu_ref[...], wd_ref[...])


def _shared_expert(xp, h1, wg, wu, wd):
    n, d = h1.shape
    f = wg.shape[1]
    tm = _tile(n, 256)
    vmem = (2 * tm * d * 2 + 4 * tm * d * 4 + 3 * d * f * 2) / MIB
    return pl.pallas_call(
        _shared_body,
        out_shape=jax.ShapeDtypeStruct((n, d), F32),
        grid=(n // tm,),
        in_specs=[
            pl.BlockSpec((tm, d // 2), lambda i: (i, 0)),
            pl.BlockSpec((tm, d), lambda i: (i, 0)),
            _const_spec((d, f)),
            _const_spec((d, f)),
            _const_spec((f, d)),
        ],
        out_specs=pl.BlockSpec((tm, d), lambda i: (i, 0)),
        compiler_params=_compiler_params(("parallel",), vmem),
        name="shared_expert",
    )(xp, h1, wg, wu, wd)


ROW_GROUP = 8


def _dispatch_body(slot_ref, xp_ref, xs_ref, sem, *, tt):
    def start(g, carry):
        base = pl.multiple_of(g * ROW_GROUP, ROW_GROUP)
        for s in range(ROW_GROUP):
            for k in range(TOP_K):
                pltpu.make_async_copy(xp_ref.at[pl.ds(base + s, 1)],
                                      xs_ref.at[pl.ds(slot_ref[(base + s) * TOP_K + k], 1)],
                                      sem).start(priority=k % DMA_PRIORITIES)
        return carry

    lax.fori_loop(0, tt // ROW_GROUP, start, 0)
    for _ in range(TOP_K):
        pltpu.make_async_copy(xp_ref, xs_ref.at[pl.ds(0, tt)], sem).wait()


def _moe_dispatch(slot_flat, xp):
    n, half = xp.shape
    tt = _tile(n, 256)
    return pl.pallas_call(
        functools.partial(_dispatch_body, tt=tt),
        out_shape=jax.ShapeDtypeStruct((n * TOP_K, half), U32),
        grid=(n // tt,),
        in_specs=[
            pl.BlockSpec((tt * TOP_K,), lambda i: (i,), memory_space=pltpu.SMEM),
            pl.BlockSpec((tt, half), lambda i: (i, 0)),
        ],
        out_specs=pl.BlockSpec(memory_space=pl.ANY),
        scratch_shapes=[pltpu.SemaphoreType.DMA(())],
        compiler_params=_compiler_params(("arbitrary",), 2 * tt * half * 4 / MIB),
        name="moe_dispatch",
    )(slot_flat, xp)


def _ffn_body(vblk_ref, vexp_ref, vlo_ref, vhi_ref, vpar_ref, nexp_ref, clo_ref, chi_ref,
              xs_ref, wg_hbm, wu_hbm, wd_hbm, ys_ref,
              wg_ref, wu_ref, wd_ref, sg_ref, su_ref, sd_ref, sems, *, rb, chunks):
    v = pl.program_id(0)
    mats = ((wg_hbm, wg_ref, sg_ref), (wu_hbm, wu_ref, su_ref), (wd_hbm, wd_ref, sd_ref))

    def chunk_copy(m, e, c, stage):
        hbm, _, stage_ref = mats[m]
        rows = stage_ref.shape[1]
        return pltpu.make_async_copy(hbm.at[e, pl.ds(c * rows, rows)], stage_ref.at[stage], sems.at[m, stage])

    def start_chunks(e, lo, hi):
        for m in range(3):
            for ahead in range(2):
                @pl.when(lo + ahead < hi)
                def _():
                    chunk_copy(m, e, lo + ahead, ahead).start()

    def convert_chunks(e, par, lo, hi):
        def convert(c, carry):
            stage = (c - lo) % 2
            for m in range(3):
                _, dst_ref, stage_ref = mats[m]
                rows = stage_ref.shape[1]
                chunk_copy(m, e, c, stage).wait()
                dst_ref[par, pl.ds(pl.multiple_of(c * rows, rows), rows), :] = stage_ref[stage].astype(BF16)

                @pl.when(c + 2 < hi)
                def _():
                    chunk_copy(m, e, c + 2, stage).start()
            return carry

        lax.fori_loop(lo, hi, convert, 0)

    @pl.when(v == 0)
    def _():
        start_chunks(vexp_ref[0], 0, chunks)
        convert_chunks(vexp_ref[0], vpar_ref[0], 0, chunks)

    start_chunks(nexp_ref[v], clo_ref[v], chi_ref[v])
    par = vpar_ref[v]
    first_visit = (v == 0) | (vblk_ref[v] != vblk_ref[jnp.maximum(v - 1, 0)])
    half_rows = rb // 2
    lo_in_block = vlo_ref[v] - vblk_ref[v] * rb
    hi_in_block = vhi_ref[v] - vblk_ref[v] * rb
    lower_only = hi_in_block <= half_rows
    upper_only = lo_in_block >= half_rows

    def visit_rows(r0, nrows):
        rs = slice(r0, r0 + nrows)
        y = _swiglu(xs_ref[rs, :], wg_ref[par], wu_ref[par], wd_ref[par])
        packed = _pack_halves(y.astype(BF16).astype(F32))

        @pl.when(first_visit)
        def _():
            ys_ref[rs, :] = packed
            if nrows < rb:
                ys_ref[rb - nrows - r0:rb - r0, :] = jnp.zeros_like(packed)

        @pl.when(jnp.logical_not(first_visit))
        def _():
            rows = vblk_ref[v] * rb + r0 + lax.broadcasted_iota(I32, packed.shape, 0)
            mine = (rows >= vlo_ref[v]) & (rows < vhi_ref[v])
            ys_ref[rs, :] = jnp.where(mine, packed, ys_ref[rs, :])

    @pl.when(lower_only)
    def _():
        visit_rows(0, half_rows)

    @pl.when(upper_only)
    def _():
        visit_rows(half_rows, half_rows)

    @pl.when(jnp.logical_not(lower_only | upper_only))
    def _():
        visit_rows(0, rb)

    convert_chunks(nexp_ref[v], 1 - par, clo_ref[v], chi_ref[v])


def _weight_chunks(f):
    chunks = 16
    while f % (chunks * V7X_BF16_SUBLANES):
        chunks //= 2
    return chunks


def _expert_ffn(visits, xs, wg, wu, wd, *, rb):
    rows, half = xs.shape
    n_exp, d, f = wg.shape
    n_visits = visits[0].shape[0]
    chunks = _weight_chunks(f)
    vmem = (2 * 3 * d * f * 2 + 3 * 2 * (d // chunks) * f * 4 + 4 * rb * half * 4) / MIB
    block = lambda v, *tables: (tables[0][v], 0)
    grid_spec = pltpu.PrefetchScalarGridSpec(
        num_scalar_prefetch=len(visits),
        grid=(n_visits,),
        in_specs=[
            pl.BlockSpec((rb, half), block),
            pl.BlockSpec(memory_space=pl.ANY),
            pl.BlockSpec(memory_space=pl.ANY),
            pl.BlockSpec(memory_space=pl.ANY),
        ],
        out_specs=pl.BlockSpec((rb, half), block),
        scratch_shapes=[
            pltpu.VMEM((2, d, f), BF16), pltpu.VMEM((2, d, f), BF16), pltpu.VMEM((2, f, d), BF16),
            pltpu.VMEM((2, d // chunks, f), F32), pltpu.VMEM((2, d // chunks, f), F32),
            pltpu.VMEM((2, f // chunks, d), F32),
            pltpu.SemaphoreType.DMA((3, 2)),
        ],
    )
    return pl.pallas_call(
        functools.partial(_ffn_body, rb=rb, chunks=chunks),
        out_shape=jax.ShapeDtypeStruct((rows, half), U32),
        grid_spec=grid_spec,
        compiler_params=_compiler_params(("arbitrary",), vmem),
        name="expert_ffn",
    )(*visits, xs, wg, wu, wd)


def _visit_tables(counts, n_rows, rb, chunks):
    n_exp = counts.shape[0]
    n_blocks = n_rows // rb
    n_visits = n_blocks + n_exp
    ends = jnp.cumsum(counts)
    starts = ends - counts
    first_blk = starts // rb
    last_blk = jnp.maximum(ends - 1, starts) // rb
    per_exp = jnp.where(counts > 0, last_blk - first_blk + 1, 0)
    vis_end = jnp.cumsum(per_exp)
    vis_start = vis_end - per_exp
    total = vis_end[-1]
    v = jnp.arange(n_visits, dtype=I32)
    e = jnp.minimum(jnp.sum((v[:, None] >= vis_end[None, :]).astype(I32), axis=1), n_exp - 1)
    is_e = e[:, None] == jnp.arange(n_exp, dtype=I32)[None, :]
    of_e = lambda table: jnp.sum(jnp.where(is_e, table[None, :], 0), axis=1)
    blk = of_e(first_blk) + (v - of_e(vis_start))
    lo = jnp.maximum(of_e(starts), blk * rb)
    hi = jnp.minimum(of_e(ends), (blk + 1) * rb)
    used = v < total
    ids = jnp.arange(n_exp, dtype=I32)
    nonempty = counts > 0
    last_e = jnp.max(jnp.where(nonempty, ids, 0))
    slot_of = (jnp.cumsum(nonempty.astype(I32)) - 1) % 2
    later = (ids[None, :] > ids[:, None]) & nonempty[None, :]
    next_of = jnp.min(jnp.where(later, ids[None, :], n_exp), axis=1)
    step_in_e = v - of_e(vis_start)
    steps_of_e = jnp.maximum(of_e(per_exp), 1)
    has_next = used & (of_e(next_of) < n_exp)
    clo = jnp.where(has_next, chunks * step_in_e // steps_of_e, 0)
    chi = jnp.where(has_next, chunks * (step_in_e + 1) // steps_of_e, 0)
    nxt = jnp.where(has_next, of_e(next_of), last_e)
    par = jnp.where(used, of_e(slot_of), jnp.sum(jnp.where(ids == last_e, slot_of, 0)))
    blk = jnp.where(used, blk, n_blocks - 1)
    e = jnp.where(used, e, last_e)
    lo = jnp.where(used, lo, 0)
    hi = jnp.where(used, hi, 0)
    return tuple(t.astype(I32) for t in (blk, e, lo, hi, par, nxt, clo, chi))


def _combine_body(slot_ref, next_slot_ref, w_ref, h2_ref, ys_ref, g_ref, o_ref, ybuf, sems, *, tt, d, tiles):
    half = d // 2
    i = pl.program_id(0)
    cur = i % 2

    def start_tile(slots, buf):
        def start(g, carry):
            base = pl.multiple_of(g * ROW_GROUP, ROW_GROUP)
            for s in range(ROW_GROUP):
                for k in range(TOP_K):
                    pltpu.make_async_copy(ys_ref.at[pl.ds(slots[(base + s) * TOP_K + k], 1)],
                                          ybuf.at[buf, k, pl.ds(base + s, 1)],
                                          sems.at[buf]).start(priority=k % DMA_PRIORITIES)
            return carry
        lax.fori_loop(0, tt // ROW_GROUP, start, 0)

    @pl.when(i == 0)
    def _():
        start_tile(slot_ref, 0)

    def wait_tile(buf):
        for k in range(TOP_K):
            pltpu.make_async_copy(ys_ref.at[pl.ds(0, tt)], ybuf.at[buf, k], sems.at[buf]).wait()

    def combine_rows(buf, rows):
        acc_lo = h2_ref[rows, :half]
        acc_hi = h2_ref[rows, half:]
        for k in range(TOP_K):
            lo, hi = _unpack_halves(ybuf[buf, k, rows, :])
            wk = w_ref[rows, k:k + 1]
            acc_lo = acc_lo + wk * lo
            acc_hi = acc_hi + wk * hi
        ss = jnp.sum(acc_lo * acc_lo, axis=-1, keepdims=True) + jnp.sum(acc_hi * acc_hi, axis=-1, keepdims=True)
        inv = lax.rsqrt(ss / d + EPS)
        o_ref[rows, :half] = acc_lo * inv * g_ref[:, :half]
        o_ref[rows, half:] = acc_hi * inv * g_ref[:, half:]

    for buf in range(2):
        @pl.when(cur == buf)
        def _():
            wait_tile(buf)
            for grp in range(tt // ROW_GROUP):
                rows = slice(grp * ROW_GROUP, (grp + 1) * ROW_GROUP)
                combine_rows(buf, rows)
                for r in range(rows.start, rows.stop):
                    for k in range(TOP_K):
                        pltpu.make_async_copy(ys_ref.at[pl.ds(next_slot_ref[r * TOP_K + k], 1)],
                                              ybuf.at[1 - buf, k, pl.ds(r, 1)],
                                              sems.at[1 - buf]).start(priority=k % DMA_PRIORITIES)

            @pl.when(i == tiles - 1)
            def _():
                wait_tile(1 - buf)


def _moe_combine(slot_flat, w_tok, h2, ys, g):
    n, d = h2.shape
    half = d // 2
    tt = _tile(n, 128)
    tiles = n // tt
    vmem = (2 * TOP_K * tt * half * 4 + 4 * tt * d * 4) / MIB
    return pl.pallas_call(
        functools.partial(_combine_body, tt=tt, d=d, tiles=tiles),
        out_shape=jax.ShapeDtypeStruct((n, d), F32),
        grid=(tiles,),
        in_specs=[
            pl.BlockSpec((tt * TOP_K,), lambda i: (i,), memory_space=pltpu.SMEM),
            pl.BlockSpec((tt * TOP_K,), lambda i: (jnp.minimum(i + 1, tiles - 1),), memory_space=pltpu.SMEM),
            pl.BlockSpec((tt, TOP_K), lambda i: (i, 0)),
            pl.BlockSpec((tt, d), lambda i: (i, 0)),
            pl.BlockSpec(memory_space=pl.ANY),
            pl.BlockSpec((1, d), lambda i: (0, 0)),
        ],
        out_specs=pl.BlockSpec((tt, d), lambda i: (i, 0)),
        scratch_shapes=[pltpu.VMEM((2, TOP_K, tt, half), U32), pltpu.SemaphoreType.DMA((2,))],
        compiler_params=_compiler_params(("arbitrary",), vmem),
        name="moe_combine",
    )(slot_flat, slot_flat, w_tok, h2, ys, g)


def _layer(h, l, p, dims):
    b, s, d = h.shape
    n = b * s
    x2 = h.reshape(n, d)
    aw, kvw, cw = dims["attn"], dims["kv"], dims["conv"]
    _, k_off, v_off, cv_off, cg_off, ga_off, gc_off = dims["src_offsets"]

    proj = _in_projection(x2, p["attn_norm_g"][l].reshape(1, d), p["w_in"][l].astype(BF16))
    attn = _window_attention(proj, p["sink_logits"][l], seq=s, attn_width=aw, kv_width=kvw,
                             k_off=k_off, v_off=v_off)
    conv = _conformer_conv(proj, p["conv_dw_w"][l], p["conv_dw_b"][l], p["conv_ln_g"][l],
                           p["conv_ln_b"][l], seq=s, a_off=cv_off, b_off=cg_off)
    merged = _gated_merge(attn, conv, p["w_o_attn"][l].astype(BF16), p["w_o_conv"][l].astype(BF16),
                          proj, ga_off=ga_off, gc_off=gc_off)
    h1 = _out_projection(merged, p["w_out"][l].astype(BF16), x2)

    n_exp = p["w_router"].shape[-1]
    xp, eidx, rank, wts, cnt = _router(
        h1, p["ffn_norm_g"][l].reshape(1, d), p["w_router"][l].T, p["router_bias"][l].reshape(n_exp, 1))
    h2 = _shared_expert(xp, h1, p["w_sh_gate"][l].astype(BF16), p["w_sh_up"][l].astype(BF16),
                        p["w_sh_down"][l].astype(BF16))

    counts = cnt[:, 0].astype(I32)
    starts = jnp.cumsum(counts) - counts
    is_e = eidx[None, :, :] == jnp.arange(n_exp, dtype=I32)[:, None, None]
    slot = rank + jnp.sum(jnp.where(is_e, starts[:, None, None], 0), axis=0)
    slot_flat = slot.T.reshape(-1)
    rb = _tile(n * TOP_K, 256)
    visits = _visit_tables(counts, n * TOP_K, rb, _weight_chunks(p["w_exp_gate"].shape[-1]))

    xs = _moe_dispatch(slot_flat, xp)
    ys = _expert_ffn(visits, xs, p["w_exp_gate"][l], p["w_exp_up"][l], p["w_exp_down"][l], rb=rb)
    return slot_flat, wts.T, h2, ys


def kernel(x, attn_norm_g, w_in, sink_logits, w_o_attn, conv_dw_w, conv_dw_b, conv_ln_g, conv_ln_b,
           w_o_conv, w_out, ffn_norm_g, w_router, router_bias, w_exp_gate, w_exp_up, w_exp_down,
           w_sh_gate, w_sh_up, w_sh_down, final_norm_g):
    b, s, d = x.shape
    depth = w_in.shape[0]
    assert depth == 1, "the final RMSNorm is fused into the only layer's MoE combine"
    aw = w_o_attn.shape[1]
    cw = conv_dw_w.shape[2]
    kvw = (w_in.shape[2] - aw - 2 * cw - 2 * d) // 2
    assert s % Q_BLOCK == 0 and WINDOW <= Q_BLOCK
    src = [0, aw, aw + kvw, aw + 2 * kvw, aw + 2 * kvw + cw, aw + 2 * kvw + 2 * cw, aw + 2 * kvw + 2 * cw + d]
    dims = {"attn": aw, "kv": kvw, "conv": cw, "src_offsets": src}
    p = dict(attn_norm_g=attn_norm_g, w_in=w_in, sink_logits=sink_logits, w_o_attn=w_o_attn,
             conv_dw_w=conv_dw_w, conv_dw_b=conv_dw_b, conv_ln_g=conv_ln_g, conv_ln_b=conv_ln_b,
             w_o_conv=w_o_conv, w_out=w_out, ffn_norm_g=ffn_norm_g, w_router=w_router,
             router_bias=router_bias, w_exp_gate=w_exp_gate, w_exp_up=w_exp_up, w_exp_down=w_exp_down,
             w_sh_gate=w_sh_gate, w_sh_up=w_sh_up, w_sh_down=w_sh_down)
    slot_flat, w_tok, h2, ys = _layer(x, 0, p, dims)
    out = _moe_combine(slot_flat, w_tok, h2, ys, final_norm_g.reshape(1, d))
    return out.reshape(b, s, d)
```
